```python
import jax, jax.numpy as jnp
from jax import lax
import numpy as np

D_MODEL = 2048
BATCH = 4
SEQ = 8192
DEPTH = 4

N_MIXERS = 2
N_RWKV = (DEPTH + 1) // 2
N_CONV = DEPTH // 2
HEAD_SIZE = 64
N_HEADS = D_MODEL // HEAD_SIZE
DECAY_LORA = 96
AAA_LORA = 96
VALUE_LORA = 64
GATE_LORA = 256
CONV_WIDTH = 31
FFN_CONV_WIDTH = 3
D_FF = ((8 * D_MODEL // 3 + 127) // 128) * 128
RMS_EPS = 1e-6
LN_EPS = 1e-5
GN_EPS = 64e-5

kernel_name = "rwkv7_conformer_conv_interleaved_adaln"


def rms_norm(z, g, eps=RMS_EPS):
    zf = z.astype(jnp.float32)
    y = zf * lax.rsqrt(jnp.mean(zf * zf, axis=-1, keepdims=True) + eps)
    return (y * g.astype(jnp.float32)).astype(z.dtype)


def layer_norm(z, g, b, eps=LN_EPS):
    zf = z.astype(jnp.float32)
    mu = jnp.mean(zf, axis=-1, keepdims=True)
    var = jnp.mean(jnp.square(zf - mu), axis=-1, keepdims=True)
    y = (zf - mu) * lax.rsqrt(var + eps)
    return (y * g.astype(jnp.float32) + b.astype(jnp.float32)).astype(z.dtype)


def modulate(h, shift, scale):
    return h * (1 + scale[:, None, :]) + shift[:, None, :]


def token_shift(z):
    return jnp.pad(z, ((0, 0), (1, 0), (0, 0)))[:, :-1]


def causal_depthwise_conv(z, w, b):
    K = w.shape[0]
    y = lax.conv_general_dilated(
        z, w[:, None, :].astype(z.dtype), window_strides=(1,),
        padding=((K - 1, 0),), dimension_numbers=("NWC", "WIO", "NWC"),
        feature_group_count=z.shape[-1])
    return y + b


def wkv7_scan(r, decay, k, v, a, b):
    Bsz, T, H, N = r.shape

    def step(S, inp):
        r_t, w_t, k_t, v_t, a_t, b_t = inp
        sa = jnp.einsum('bhij,bhj->bhi', S, a_t)
        S = (S * w_t[:, :, None, :] + sa[..., None] * b_t[:, :, None, :]
             + v_t[..., None] * k_t[:, :, None, :])
        y = jnp.einsum('bhij,bhj->bhi', S, r_t)
        return S, y

    seq_major = lambda z: jnp.moveaxis(z, 1, 0)
    S0 = jnp.zeros((Bsz, H, N, N), jnp.float32)
    _, ys = lax.scan(step, S0, tuple(seq_major(z) for z in (r, decay, k, v, a, b)))
    return jnp.moveaxis(ys, 0, 1)


def rwkv7_time_mix(h, mu, w_rkv, w0, w1, w2, a0, a1, a2, g1, g2, k_k, k_a, r_k,
                   lnx_g, lnx_b, w_o, v_first=None, v0=None, v1=None, v2=None):
    Bsz, T, C = h.shape
    f32 = jnp.float32
    xx = token_shift(h) - h
    x_rkv = h[None] + xx[None] * mu[:3, None, None, :]
    r, k, v = jnp.einsum('nbtc,ncd->nbtd', x_rkv, w_rkv)
    xv = x_rkv[2]
    xw, xa, xg = (h + xx * mu[n] for n in (3, 4, 5))
    log_w = -jax.nn.softplus(-(w0 + jnp.tanh(xw @ w1) @ w2)) - 0.5
    a = jax.nn.sigmoid(a0 + (xa @ a1) @ a2)
    g = jax.nn.sigmoid(xg @ g1) @ g2
    if v_first is not None:
        v = v + (v_first - v) * jax.nn.sigmoid(v0 + (xv @ v1) @ v2)
    heads = lambda z: z.reshape(Bsz, T, N_HEADS, HEAD_SIZE)
    kk = heads(k * k_k).astype(f32)
    kk = kk / jnp.maximum(jnp.linalg.norm(kk, axis=-1, keepdims=True), 1e-12)
    k = k * (1 + (a - 1) * k_a)
    rh, kh, vh, ah = heads(r), heads(k), heads(v), heads(a)
    decay = jnp.exp(-jnp.exp(heads(log_w).astype(f32)))
    y = wkv7_scan(rh.astype(f32), decay, kh.astype(f32), vh.astype(f32),
                  -kk, kk * ah.astype(f32))
    mean = jnp.mean(y, axis=-1, keepdims=True)
    var = jnp.mean(jnp.square(y - mean), axis=-1, keepdims=True)
    y = ((y - mean) * lax.rsqrt(var + GN_EPS)).reshape(Bsz, T, C)
    y = (y * lnx_g.astype(f32) + lnx_b.astype(f32)).astype(h.dtype)
    bonus = (jnp.sum(rh * kh * r_k, axis=-1, keepdims=True) * vh).reshape(Bsz, T, C)
    return ((y + bonus) * g) @ w_o, v


def conformer_conv_module(h, w_pw1, b_pw1, w_dw, b_dw, ln_g, ln_b, w_pw2, b_pw2):
    u = jax.nn.glu(h @ w_pw1 + b_pw1, axis=-1)
    u = causal_depthwise_conv(u, w_dw, b_dw)
    u = jax.nn.silu(layer_norm(u, ln_g, ln_b))
    return u @ w_pw2 + b_pw2


def conv_glu_ffn(h, w_up, w_dw, b_dw, w_down):
    gate, val = jnp.split(h @ w_up, 2, axis=-1)
    gate = causal_depthwise_conv(gate, w_dw, b_dw)
    return (jax.nn.silu(gate) * val) @ w_down


def setup_inputs(seed: int = 0) -> dict:
    key = jax.random.key(seed)
    ks = iter(jax.random.split(key, 64))
    C, F, H, N = D_MODEL, D_FF, N_HEADS, HEAD_SIZE

    def nrm(shape, std):
        return std * jax.random.normal(next(ks), shape, jnp.float32)

    def unif(shape, lo, hi):
        return jax.random.uniform(next(ks), shape, jnp.float32, lo, hi)

    return {
        "x": nrm((BATCH, SEQ, C), 1.0),
        "c": nrm((BATCH, C), 1.0),
        "ada_w": nrm((DEPTH, C, 6 * C), 0.5 * C ** -0.5),
        "ada_b": nrm((DEPTH, 6 * C), 0.01),
        "norm_mix_g": 1.0 + nrm((DEPTH, C), 0.02),
        "norm_ffn_g": 1.0 + nrm((DEPTH, C), 0.02),
        "rwkv_mu": unif((N_RWKV, 6, C), 0.0, 1.0),
        "rwkv_w_rkv": nrm((N_RWKV, 3, C, C), C ** -0.5),
        "rwkv_w0": unif((N_RWKV, C), -6.0, -1.0),
        "rwkv_w1": nrm((N_RWKV, C, DECAY_LORA), C ** -0.5),
        "rwkv_w2": nrm((N_RWKV, DECAY_LORA, C), 0.5 * DECAY_LORA ** -0.5),
        "rwkv_a0": nrm((N_RWKV, C), 0.5),
        "rwkv_a1": nrm((N_RWKV, C, AAA_LORA), C ** -0.5),
        "rwkv_a2": nrm((N_RWKV, AAA_LORA, C), 0.5 * AAA_LORA ** -0.5),
        "rwkv_v0": nrm((N_RWKV - 1, C), 0.5),
        "rwkv_v1": nrm((N_RWKV - 1, C, VALUE_LORA), C ** -0.5),
        "rwkv_v2": nrm((N_RWKV - 1, VALUE_LORA, C), 0.5 * VALUE_LORA ** -0.5),
        "rwkv_g1": nrm((N_RWKV, C, GATE_LORA), C ** -0.5),
        "rwkv_g2": nrm((N_RWKV, GATE_LORA, C), GATE_LORA ** -0.5),
        "rwkv_k_k": 0.85 + nrm((N_RWKV, C), 0.05),
        "rwkv_k_a": 1.0 + nrm((N_RWKV, C), 0.05),
        "rwkv_r_k": nrm((N_RWKV, H, N), 0.1),
        "rwkv_lnx_g": 1.0 + nrm((N_RWKV, C), 0.02),
        "rwkv_lnx_b": nrm((N_RWKV, C), 0.01),
        "rwkv_w_o": nrm((N_RWKV, C, C), C ** -0.5),
        "conv_w_pw1": nrm((N_CONV, C, 2 * C), C ** -0.5),
        "conv_b_pw1": nrm((N_CONV, 2 * C), 0.01),
        "conv_w_dw": nrm((N_CONV, CONV_WIDTH, C), CONV_WIDTH ** -0.5),
        "conv_b_dw": nrm((N_CONV, C), 0.01),
        "conv_ln_g": 1.0 + nrm((N_CONV, C), 0.02),
        "conv_ln_b": nrm((N_CONV, C), 0.01),
        "conv_w_pw2": nrm((N_CONV, C, C), C ** -0.5),
        "conv_b_pw2": nrm((N_CONV, C), 0.01),
        "ffn_w_up": nrm((DEPTH, C, 2 * F), C ** -0.5),
        "ffn_w_dw": nrm((DEPTH, FFN_CONV_WIDTH, F), FFN_CONV_WIDTH ** -0.5),
        "ffn_b_dw": nrm((DEPTH, F), 0.01),
        "ffn_w_down": nrm((DEPTH, F, C), F ** -0.5),
        "final_norm_g": 1.0 + nrm((C,), 0.02),
    }


def reference(x, c, ada_w, ada_b, norm_mix_g, norm_ffn_g,
              rwkv_mu, rwkv_w_rkv, rwkv_w0, rwkv_w1, rwkv_w2, rwkv_a0, rwkv_a1, rwkv_a2,
              rwkv_v0, rwkv_v1, rwkv_v2, rwkv_g1, rwkv_g2, rwkv_k_k, rwkv_k_a, rwkv_r_k,
              rwkv_lnx_g, rwkv_lnx_b, rwkv_w_o,
              conv_w_pw1, conv_b_pw1, conv_w_dw, conv_b_dw, conv_ln_g, conv_ln_b,
              conv_w_pw2, conv_b_pw2,
              ffn_w_up, ffn_w_dw, ffn_b_dw, ffn_w_down, final_norm_g):
    c_act = jax.nn.silu(c)
    v_first = None
    for i in range(DEPTH):
        mod = c_act @ ada_w[i] + ada_b[i]
        sh_m, sc_m, g_m, sh_f, sc_f, g_f = jnp.split(mod, 6, axis=-1)
        h = modulate(rms_norm(x, norm_mix_g[i]), sh_m, sc_m)
        j = i // N_MIXERS
        if i % N_MIXERS == 0:
            if v_first is None:
                out, v_first = rwkv7_time_mix(
                    h, rwkv_mu[j], rwkv_w_rkv[j], rwkv_w0[j], rwkv_w1[j], rwkv_w2[j],
                    rwkv_a0[j], rwkv_a1[j], rwkv_a2[j], rwkv_g1[j], rwkv_g2[j],
                    rwkv_k_k[j], rwkv_k_a[j], rwkv_r_k[j], rwkv_lnx_g[j], rwkv_lnx_b[j],
                    rwkv_w_o[j])
            else:
                out, _ = rwkv7_time_mix(
                    h, rwkv_mu[j], rwkv_w_rkv[j], rwkv_w0[j], rwkv_w1[j], rwkv_w2[j],
                    rwkv_a0[j], rwkv_a1[j], rwkv_a2[j], rwkv_g1[j], rwkv_g2[j],
                    rwkv_k_k[j], rwkv_k_a[j], rwkv_r_k[j], rwkv_lnx_g[j], rwkv_lnx_b[j],
                    rwkv_w_o[j], v_first, rwkv_v0[j - 1], rwkv_v1[j - 1], rwkv_v2[j - 1])
        else:
            out = conformer_conv_module(
                h, conv_w_pw1[j], conv_b_pw1[j], conv_w_dw[j], conv_b_dw[j],
                conv_ln_g[j], conv_ln_b[j], conv_w_pw2[j], conv_b_pw2[j])
        x = x + g_m[:, None, :] * out
        h = modulate(rms_norm(x, norm_ffn_g[i]), sh_f, sc_f)
        x = x + g_f[:, None, :] * conv_glu_ffn(h, ffn_w_up[i], ffn_w_dw[i], ffn_b_dw[i], ffn_w_down[i])
    return rms_norm(x, final_norm_g)
```

```python
import functools

import jax
import jax.numpy as jnp
from jax import lax
from jax.experimental import pallas as pl
from jax.experimental.pallas import tpu as pltpu

F32 = jnp.float32
BF16 = jnp.bfloat16

HEAD_SIZE = 64
RMS_EPS = 1e-6
LN_EPS = 1e-5
GN_EPS = 64e-5
CONV_WIDTH = 31
FFN_CONV_WIDTH = 3

LANES = 128
SUBLANES_F32 = 8
SUBLANES_BF16 = 16
MXU_DIM = 256
VMEM_LIMIT_BYTES = 56 * 1024 * 1024

WKV_CHUNK = HEAD_SIZE
QUAD = MXU_DIM
HEADS_PER_QUAD = QUAD // HEAD_SIZE
CONV_HALO = 32
FFN_HALO = 16


def _cparams(semantics):
    return pltpu.CompilerParams(dimension_semantics=semantics,
                                vmem_limit_bytes=VMEM_LIMIT_BYTES)


def _sigmoid(z):
    return 1.0 / (1.0 + jnp.exp(-z))


def _dot(a, b):
    return jnp.dot(a.astype(BF16), b.astype(BF16), preferred_element_type=F32)


def _dot_nt(a, b):
    return lax.dot_general(a.astype(BF16), b.astype(BF16), (((1,), (1,)), ((), ())),
                           preferred_element_type=F32)


def _dot_tn(a, b):
    return lax.dot_general(a.astype(BF16), b.astype(BF16), (((0,), (0,)), ((), ())),
                           preferred_element_type=F32)


def _split2(z):
    hi = z.astype(BF16)
    lo = (z - hi.astype(F32)).astype(BF16)
    return hi, lo


def _split3(z):
    hi = z.astype(BF16)
    r1 = z - hi.astype(F32)
    mid = r1.astype(BF16)
    lo = (r1 - mid.astype(F32)).astype(BF16)
    return hi, mid, lo


def _mod_kernel(c_ref, w_ref, b_ref, o_ref):
    c = c_ref[...]
    c_act = (c * _sigmoid(c)).astype(BF16)
    o_ref[0] = jnp.dot(c_act, w_ref[0].astype(BF16), preferred_element_type=F32) + b_ref[0]


def _adaln_mod(c_pad, ada_w, ada_b, *, tn):
    depth, C, n6 = ada_w.shape
    rows = c_pad.shape[0]
    return pl.pallas_call(
        _mod_kernel,
        out_shape=jax.ShapeDtypeStruct((depth, rows, n6), F32),
        grid=(depth, n6 // tn),
        in_specs=[pl.BlockSpec((rows, C), lambda l, j: (0, 0)),
                  pl.BlockSpec((1, C, tn), lambda l, j: (l, 0, j)),
                  pl.BlockSpec((1, 1, tn), lambda l, j: (l, 0, j))],
        out_specs=pl.BlockSpec((1, rows, tn), lambda l, j: (l, 0, j)),
        compiler_params=_cparams(("parallel", "parallel")),
        name="adaln_mod",
    )(c_pad, ada_w, ada_b.reshape(depth, 1, n6))


def _rms_mod(x, g, mod_ref, rows):
    y = x * lax.rsqrt(jnp.mean(x * x, axis=-1, keepdims=True) + RMS_EPS) * g
    if rows is not None:
        shift_row, scale_row = rows
        y = y * (1.0 + mod_ref[0, scale_row:scale_row + 1, :]) + mod_ref[0, shift_row:shift_row + 1, :]
    return y


def _norm_kernel(x_ref, g_ref, mod_ref, o_ref, *, rows):
    o_ref[...] = _rms_mod(x_ref[...], g_ref[...], mod_ref, rows).astype(o_ref.dtype)


def _norm_mod(x, g, mod_l, *, rows, out_dtype, tm, seq_len):
    M, C = x.shape
    tiles_per_seq = seq_len // tm
    return pl.pallas_call(
        functools.partial(_norm_kernel, rows=rows),
        out_shape=jax.ShapeDtypeStruct((M, C), out_dtype),
        grid=(M // tm,),
        in_specs=[pl.BlockSpec((tm, C), lambda i: (i, 0)),
                  pl.BlockSpec((1, C), lambda i: (0, 0)),
                  pl.BlockSpec((1,) + mod_l.shape[1:], lambda i: (i // tiles_per_seq, 0, 0))],
        out_specs=pl.BlockSpec((tm, C), lambda i: (i, 0)),
        compiler_params=_cparams(("parallel",)),
        name="norm_mod",
    )(x, g.reshape(1, C), mod_l)


def _rwkv_in_kernel(x_ref, xh_ref, g_ref, mod_ref, mu_ref, o_ref, *, rows, tiles_per_seq):
    i = pl.program_id(0)
    g = g_ref[...]
    h = _rms_mod(x_ref[...], g, mod_ref, rows)
    h_halo = _rms_mod(xh_ref[...], g, mod_ref, rows)
    at_seq_start = (i % tiles_per_seq) == 0
    prev_row = jnp.where(at_seq_start, 0.0, h_halo[SUBLANES_F32 - 1:SUBLANES_F32, :])
    rolled = pltpu.roll(h, 1, axis=0)
    row = lax.broadcasted_iota(jnp.int32, h.shape, 0)
    xx = jnp.where(row == 0, prev_row, rolled) - h
    for n in range(o_ref.shape[0]):
        o_ref[n] = (h + xx * mu_ref[n:n + 1, :]).astype(o_ref.dtype)


def _rwkv_in(x, g, mod_l, mu, *, rows, tm, seq_len):
    M, C = x.shape
    tiles_per_seq = seq_len // tm
    halo_blocks_per_tile = tm // SUBLANES_F32
    n_mix = mu.shape[0]
    return pl.pallas_call(
        functools.partial(_rwkv_in_kernel, rows=rows, tiles_per_seq=tiles_per_seq),
        out_shape=jax.ShapeDtypeStruct((n_mix, M, C), BF16),
        grid=(M // tm,),
        in_specs=[pl.BlockSpec((tm, C), lambda i: (i, 0)),
                  pl.BlockSpec((SUBLANES_F32, C),
                               lambda i: (jnp.maximum(i * halo_blocks_per_tile - 1, 0), 0)),
                  pl.BlockSpec((1, C), lambda i: (0, 0)),
                  pl.BlockSpec((1,) + mod_l.shape[1:], lambda i: (i // tiles_per_seq, 0, 0)),
                  pl.BlockSpec((n_mix, C), lambda i: (0, 0))],
        out_specs=pl.BlockSpec((n_mix, tm, C), lambda i: (0, i, 0)),
        compiler_params=_cparams(("parallel",)),
        name="rwkv_in",
    )(x, x, g.reshape(1, C), mod_l, mu)


def _gemm_kernel(*refs, has_bias, gate_row):
    x_ref, w_ref = refs[0], refs[1]
    pos = 2
    acc = jnp.dot(x_ref[0], w_ref[0], preferred_element_type=F32)
    if has_bias:
        acc = acc + refs[pos][0]
        pos += 1
    if gate_row is not None:
        res_ref, mod_ref = refs[pos], refs[pos + 1]
        pos += 2
        acc = res_ref[...] + mod_ref[0, gate_row:gate_row + 1, :] * acc
    o_ref = refs[pos]
    o_ref[0] = acc.astype(o_ref.dtype)


def _gemm(x, w, *, bias=None, res=None, mod_l=None, gate_row=None, out_dtype, tm, tn, seq_len=None):
    nx, M, K = x.shape
    nb, _, N = w.shape
    in_specs = [pl.BlockSpec((1, tm, K), (lambda n, i, j: (n, i, 0)) if nx > 1 else (lambda n, i, j: (0, i, 0))),
                pl.BlockSpec((1, K, tn), lambda n, i, j: (n, 0, j))]
    args = [x, w]
    if bias is not None:
        in_specs.append(pl.BlockSpec((1, 1, tn), lambda n, i, j: (n, 0, j)))
        args.append(bias.reshape(nb, 1, N).astype(F32))
    if gate_row is not None:
        tiles_per_seq = seq_len // tm
        in_specs.append(pl.BlockSpec((tm, tn), lambda n, i, j: (i, j)))
        in_specs.append(pl.BlockSpec((1, mod_l.shape[1], tn), lambda n, i, j: (i // tiles_per_seq, 0, j)))
        args += [res, mod_l]
    return pl.pallas_call(
        functools.partial(_gemm_kernel, has_bias=bias is not None, gate_row=gate_row),
        out_shape=jax.ShapeDtypeStruct((nb, M, N), out_dtype),
        grid=(nb, M // tm, N // tn),
        in_specs=in_specs,
        out_specs=pl.BlockSpec((1, tm, tn), lambda n, i, j: (n, i, j)),
        compiler_params=_cparams(("parallel", "parallel", "parallel")),
        name="gemm",
    )(*args)


def _lora_kernel(*refs, has_v):
    (xw_ref, xa_ref, xg_ref, w1_ref, w2_ref, a1_ref, a2_ref, g1_ref, g2_ref, w0_ref, a0_ref) = refs[:11]
    pos = 11
    if has_v:
        xv_ref, v1_ref, v2_ref, v0_ref, v_ref, vf_ref = refs[pos:pos + 6]
        pos += 6
    lw_ref, a_ref, g_ref = refs[pos:pos + 3]
    pos += 3

    zw = w0_ref[...] + _dot(jnp.tanh(_dot(xw_ref[0], w1_ref[...])), w2_ref[...])
    neg = -zw
    softplus = jnp.maximum(neg, 0.0) + jnp.log(1.0 + jnp.exp(-jnp.abs(neg)))
    lw_ref[...] = -jnp.exp(-softplus - 0.5)
    a_ref[...] = _sigmoid(a0_ref[...] + _dot(_dot(xa_ref[0], a1_ref[...]), a2_ref[...]))
    g_ref[...] = _dot(_sigmoid(_dot(xg_ref[0], g1_ref[...])), g2_ref[...]).astype(g_ref.dtype)
    if has_v:
        vo_ref = refs[pos]
        mix = _sigmoid(v0_ref[...] + _dot(_dot(xv_ref[0], v1_ref[...]), v2_ref[...]))
        v = v_ref[0]
        vo_ref[...] = v + (vf_ref[...] - v) * mix


def _rwkv_lora(xs, w1, w2, a1, a2, g1, g2, w0, a0, *, vmix=None, tm):
    _, M, C = xs.shape
    row = lambda n: pl.BlockSpec((1, tm, C), lambda i, n=n: (n, i, 0))
    full = lambda arr: pl.BlockSpec(arr.shape, lambda i: (0,) * arr.ndim)
    tile = pl.BlockSpec((tm, C), lambda i: (i, 0))
    vec = lambda z: z.reshape(1, C)
    args = [xs, xs, xs, w1, w2, a1, a2, g1, g2, vec(w0), vec(a0)]
    in_specs = [row(3), row(4), row(5)] + [full(z) for z in args[3:]]
    out_shape = [jax.ShapeDtypeStruct((M, C), F32), jax.ShapeDtypeStruct((M, C), F32),
                 jax.ShapeDtypeStruct((M, C), BF16)]
    out_specs = [tile, tile, tile]
    if vmix is not None:
        v1, v2, v0, rkv, v_first = vmix
        extra = [xs, v1, v2, vec(v0), rkv, v_first]
        args += extra
        in_specs += [row(2), full(v1), full(v2), full(extra[3]),
                     pl.BlockSpec((1, tm, C), lambda i: (2, i, 0)), tile]
        out_shape.append(jax.ShapeDtypeStruct((M, C), F32))
        out_specs.append(tile)
    return pl.pallas_call(
        functools.partial(_lora_kernel, has_v=vmix is not None),
        out_shape=out_shape,
        grid=(M // tm,),
        in_specs=in_specs,
        out_specs=out_specs,
        compiler_params=_cparams(("parallel",)),
        name="rwkv_lora",
    )(*args)


def _wkv_masks():
    L, Q = WKV_CHUNK, QUAD
    idx = jnp.arange(Q)
    same_block = (idx[:, None] // L) == (idx[None, :] // L)
    tri = (jnp.arange(L)[:, None] >= jnp.arange(L)[None, :]).astype(BF16)
    strict = (same_block & (idx[:, None] > idx[None, :])).astype(F32)
    incl = (same_block & (idx[:, None] >= idx[None, :])).astype(F32)
    block = same_block.astype(F32)
    heads = (jnp.arange(HEADS_PER_QUAD)[:, None] == (idx[None, :] // HEAD_SIZE)).astype(F32)
    heads = jnp.concatenate([heads, jnp.zeros((SUBLANES_F32 - HEADS_PER_QUAD, Q), F32)], axis=0)
    return tri, strict, incl, block, heads


def _mask_stack(z, heads_ref):
    return jnp.concatenate([z * heads_ref[h:h + 1, :] for h in range(HEADS_PER_QUAD)], axis=0)


def _row_block_sum(z):
    L = WKV_CHUNK
    out = z[0:L]
    for h in range(1, HEADS_PER_QUAD):
        out = out + z[h * L:(h + 1) * L]
    return out


def _tile_rows(z):
    return jnp.concatenate([z] * HEADS_PER_QUAD, axis=0)


def _head_sums(z, block_bf16):
    hi, lo = _split2(z)
    return (jnp.dot(hi, block_bf16, preferred_element_type=F32)
            + jnp.dot(lo, block_bf16, preferred_element_type=F32))


def _wkv_par_kernel(r_ref, k_ref, v_ref, lw_ref, a_ref, kk_w_ref, ka_w_ref, rk_w_ref,
                    tri_ref, strict_ref, incl_ref, block_ref, heads_ref,
                    rp_ref, p2_ref, g_out_ref, h_out_ref, wl_ref, bonus_ref, *, chunks):
    L = WKV_CHUNK
    tri = tri_ref[...]
    block = block_ref[...]
    block_bf16 = block.astype(BF16)
    strict = strict_ref[...] > 0.5
    incl = incl_ref[...] > 0.5
    on_block = block > 0.5
    eye = jnp.where(strict, 0.0, incl_ref[...])

    for c in range(chunks):
        rows = slice(c * L, (c + 1) * L)
        r = r_ref[0, rows, :]
        k = k_ref[0, rows, :]
        v = v_ref[rows, :]
        lw = lw_ref[rows, :]
        a_sig = a_ref[rows, :]

        hi, mid, lo = _split3(lw)
        cum = (jnp.dot(tri, hi, preferred_element_type=F32)
               + jnp.dot(tri, mid, preferred_element_type=F32)
               + jnp.dot(tri, lo, preferred_element_type=F32))
        cum_end = cum[L - 1:L, :]
        w_incl = jnp.exp(cum)
        w_excl = jnp.exp(cum - lw)
        w_inv = jnp.exp(-cum)
        w_to_end = jnp.exp(cum_end - cum)

        k_raw = k * kk_w_ref[...]
        kk = k_raw / jnp.maximum(jnp.sqrt(_head_sums(k_raw * k_raw, block_bf16)), 1e-12)
        k_mod = k * (1.0 + (a_sig - 1.0) * ka_w_ref[...])
        b = kk * a_sig

        a_t = -kk * w_excl
        b_t = b * w_inv
        k_t = k_mod * w_inv
        r_t = r * w_incl
        b_end = b * w_to_end
        k_end = k_mod * w_to_end

        lhs = jnp.concatenate([_mask_stack(a_t, heads_ref), _mask_stack(r_t, heads_ref)], axis=0).astype(BF16)
        by_b = _dot_nt(lhs, _tile_rows(b_t))
        by_k = _dot_nt(lhs, _tile_rows(k_t))
        n_ab = jnp.where(strict, by_b[:4 * L], 0.0)
        n_ak = jnp.where(strict, by_k[:4 * L], 0.0)
        n_rb = jnp.where(incl, by_b[4 * L:], 0.0)
        n_rk = jnp.where(incl, by_k[4 * L:], 0.0)

        t_inv = eye + n_ab
        power = n_ab
        span = 2
        while span < L:
            power = _dot(power, power)
            t_inv = t_inv + _dot(t_inv, power)
            span *= 2

        v_stack = _mask_stack(v, heads_ref)
        p1 = _row_block_sum(_dot(n_ak, v_stack))
        a_p = _row_block_sum(_dot(t_inv, _mask_stack(a_t, heads_ref)))
        p1_p = _row_block_sum(_dot(t_inv, _mask_stack(p1, heads_ref)))
        r_p = r_t + _row_block_sum(_dot(n_rb, _mask_stack(a_p, heads_ref)))
        p2 = (_row_block_sum(_dot(n_rb, _mask_stack(p1_p, heads_ref)))
              + _row_block_sum(_dot(n_rk, v_stack)))
        g_map = _row_block_sum(jnp.where(on_block, _dot_tn(a_p, b_end), 0.0))
        h_map = _row_block_sum(jnp.where(on_block, _dot_tn(p1_p, b_end) + _dot_tn(v, k_end), 0.0))

        rp_ref[rows, :] = r_p.astype(rp_ref.dtype)
        p2_ref[rows, :] = p2
        g_out_ref[rows, :] = g_map.astype(g_out_ref.dtype)
        h_out_ref[rows, :] = h_map
        wl_ref[c] = jnp.exp(cum_end)
        bonus_ref[rows, :] = _head_sums(r * k_mod * rk_w_ref[...], block_bf16) * v


def _wkv_par(rkv, v, lw, a_sig, k_k, k_a, r_k, masks, *, chunks):
    _, M, C = rkv.shape
    L, Q = WKV_CHUNK, QUAD
    rows = chunks * L
    tile = pl.BlockSpec((rows, Q), lambda i, q: (i, q))
    rk_tile = lambda n: pl.BlockSpec((1, rows, Q), lambda i, q, n=n: (n, i, q))
    vec = pl.BlockSpec((1, Q), lambda i, q: (0, q))
    const = lambda arr: pl.BlockSpec(arr.shape, lambda i, q: (0,) * arr.ndim)
    out_shape = [jax.ShapeDtypeStruct((M, C), BF16), jax.ShapeDtypeStruct((M, C), F32),
                 jax.ShapeDtypeStruct((M, C), BF16), jax.ShapeDtypeStruct((M, C), F32),
                 jax.ShapeDtypeStruct((M // L, 1, C), F32), jax.ShapeDtypeStruct((M, C), F32)]
    out_specs = [tile, tile, tile, tile, pl.BlockSpec((chunks, 1, Q), lambda i, q: (i, 0, q)), tile]
    return pl.pallas_call(
        functools.partial(_wkv_par_kernel, chunks=chunks),
        out_shape=out_shape,
        grid=(M // rows, C // Q),
        in_specs=[rk_tile(0), rk_tile(1), tile, tile, tile, vec, vec, vec] + [const(m) for m in masks],
        out_specs=out_specs,
        compiler_params=_cparams(("parallel", "parallel")),
        name="wkv_par",
    )(rkv, rkv, v, lw, a_sig, k_k.reshape(1, C), k_a.reshape(1, C), r_k.reshape(1, C), *masks)


def _wkv_seq_kernel(rp_ref, p2_ref, gm_ref, hm_ref, wl_ref, bonus_ref, gate_ref, lng_ref, lnb_ref,
                    block_ref, o_ref, s_ref, *, chunks, quads):
    L, Q = WKV_CHUNK, QUAD

    @pl.when(pl.program_id(1) == 0)
    def _():
        s_ref[...] = jnp.zeros_like(s_ref)

    block = block_ref[...]
    on_block = block > 0.5
    block_bf16 = block.astype(BF16)
    inv_n = 1.0 / HEAD_SIZE

    def chunk_body(c, carry):
        rows = pl.ds(pl.multiple_of(c * L, L), L)
        for q in range(quads):
            lanes = slice(q * Q, (q + 1) * Q)
            s = s_ref[q]
            s_block = jnp.where(on_block, _tile_rows(s), 0.0).astype(BF16)
            y = _dot_nt(rp_ref[rows, lanes], s_block) + p2_ref[rows, lanes]
            g_block = jnp.where(on_block, _tile_rows(gm_ref[rows, lanes].astype(F32)), 0.0)
            s_ref[q] = s * wl_ref[c, :, lanes] + _dot(s, g_block) + hm_ref[rows, lanes]

            mean = _head_sums(y, block_bf16) * inv_n
            d = y - mean
            var = _head_sums(d * d, block_bf16) * inv_n
            yn = d * lax.rsqrt(var + GN_EPS) * lng_ref[:, lanes] + lnb_ref[:, lanes]
            out = (yn + bonus_ref[rows, lanes]) * gate_ref[rows, lanes].astype(F32)
            o_ref[rows, lanes] = out.astype(o_ref.dtype)
        return carry

    lax.fori_loop(0, chunks, chunk_body, 0)


def _wkv_seq(rp, p2, gm, hm, wl, bonus, gate, lnx_g, lnx_b, block, *, chunks, seq_len):
    M, C = rp.shape
    L, Q = WKV_CHUNK, QUAD
    rows = chunks * L
    steps_per_seq = seq_len // rows
    quads = C // Q
    tile = pl.BlockSpec((rows, C), lambda b, i: (b * steps_per_seq + i, 0))
    vec = pl.BlockSpec((1, C), lambda b, i: (0, 0))
    return pl.pallas_call(
        functools.partial(_wkv_seq_kernel, chunks=chunks, quads=quads),
        out_shape=jax.ShapeDtypeStruct((M, C), BF16),
        grid=(M // seq_len, steps_per_seq),
        in_specs=[tile, tile, tile, tile,
                  pl.BlockSpec((chunks, 1, C), lambda b, i: (b * steps_per_seq + i, 0, 0)),
                  tile, tile, vec, vec,
                  pl.BlockSpec(block.shape, lambda b, i: (0, 0))],
        out_specs=tile,
        scratch_shapes=[pltpu.VMEM((quads, L, Q), F32)],
        compiler_params=_cparams(("parallel", "arbitrary")),
        name="wkv_seq",
    )(rp, p2, gm, hm, wl, bonus, gate, lnx_g.reshape(1, C), lnx_b.reshape(1, C), block)


def _conf_mid_kernel(a_ref, b_ref, ah_ref, bh_ref, wdw_ref, bdw_ref, lng_ref, lnb_ref, o_ref,
                     u_scr, c_scr, *, tiles_per_seq):
    tm, C = o_ref.shape
    at_seq_start = (pl.program_id(0) % tiles_per_seq) == 0
    u_halo = ah_ref[0].astype(F32) * _sigmoid(bh_ref[0].astype(F32))
    u_scr[0:CONV_HALO, :] = jnp.where(at_seq_start, 0.0, u_halo)
    u_scr[CONV_HALO:CONV_HALO + tm, :] = a_ref[0].astype(F32) * _sigmoid(b_ref[0].astype(F32))
    first_tap = CONV_HALO - (CONV_WIDTH - 1)

    def slab(s, carry):
        lanes = pl.ds(pl.multiple_of(s * LANES, LANES), LANES)
        acc = jnp.zeros((tm, LANES), F32) + bdw_ref[:, lanes]
        for tap in range(CONV_WIDTH):
            acc = acc + u_scr[pl.ds(first_tap + tap, tm), lanes] * wdw_ref[tap:tap + 1, lanes]
        c_scr[:, lanes] = acc
        return carry

    lax.fori_loop(0, C // LANES, slab, 0)
    y = c_scr[...]
    mu = jnp.mean(y, axis=-1, keepdims=True)
    d = y - mu
    var = jnp.mean(d * d, axis=-1, keepdims=True)
    yn = d * lax.rsqrt(var + LN_EPS) * lng_ref[...] + lnb_ref[...]
    o_ref[...] = (yn * _sigmoid(yn)).astype(o_ref.dtype)


def _conf_mid(p, w_dw, b_dw, ln_g, ln_b, *, tm, seq_len):
    _, M, C2 = p.shape
    C = C2 // 2
    tiles_per_seq = seq_len // tm
    halo_per_tile = tm // CONV_HALO
    halo_idx = lambda i: jnp.maximum(i * halo_per_tile - 1, 0)
    vec = pl.BlockSpec((1, C), lambda i: (0, 0))
    return pl.pallas_call(
        functools.partial(_conf_mid_kernel, tiles_per_seq=tiles_per_seq),
        out_shape=jax.ShapeDtypeStruct((M, C), BF16),
        grid=(M // tm,),
        in_specs=[pl.BlockSpec((1, tm, C), lambda i: (0, i, 0)),
                  pl.BlockSpec((1, tm, C), lambda i: (0, i, 1)),
                  pl.BlockSpec((1, CONV_HALO, C), lambda i: (0, halo_idx(i), 0)),
                  pl.BlockSpec((1, CONV_HALO, C), lambda i: (0, halo_idx(i), 1)),
                  pl.BlockSpec(w_dw.shape, lambda i: (0, 0)), vec, vec, vec],
        out_specs=pl.BlockSpec((tm, C), lambda i: (i, 0)),
        scratch_shapes=[pltpu.VMEM((CONV_HALO + tm, C), F32), pltpu.VMEM((tm, C), F32)],
        compiler_params=_cparams(("parallel",)),
        name="conformer_mid",
    )(p, p, p, p, w_dw, b_dw.reshape(1, C), ln_g.reshape(1, C), ln_b.reshape(1, C))


def _ffn_mid_kernel(g_ref, v_ref, gh_ref, wdw_ref, bdw_ref, o_ref, g_scr, *, tiles_per_seq):
    tm, tn = o_ref.shape
    at_seq_start = (pl.program_id(0) % tiles_per_seq) == 0
    g_scr[0:FFN_HALO, :] = jnp.where(at_seq_start, 0.0, gh_ref[0].astype(F32))
    g_scr[FFN_HALO:FFN_HALO + tm, :] = g_ref[0].astype(F32)
    first_tap = FFN_HALO - (FFN_CONV_WIDTH - 1)
    acc = jnp.zeros((tm, tn), F32) + bdw_ref[...]
    for tap in range(FFN_CONV_WIDTH):
        acc = acc + g_scr[pl.ds(first_tap + tap, tm), :] * wdw_ref[tap:tap + 1, :]
    o_ref[...] = (acc * _sigmoid(acc) * v_ref[0].astype(F32)).astype(o_ref.dtype)


def _ffn_mid(gv, w_dw, b_dw, *, tm, tn, seq_len):
    _, M, F = gv.shape
    tiles_per_seq = seq_len // tm
    halo_per_tile = tm // FFN_HALO
    return pl.pallas_call(
        functools.partial(_ffn_mid_kernel, tiles_per_seq=tiles_per_seq),
        out_shape=jax.ShapeDtypeStruct((M, F), BF16),
        grid=(M // tm, F // tn),
        in_specs=[pl.BlockSpec((1, tm, tn), lambda i, j: (0, i, j)),
                  pl.BlockSpec((1, tm, tn), lambda i, j: (1, i, j)),
                  pl.BlockSpec((1, FFN_HALO, tn), lambda i, j: (0, jnp.maximum(i * halo_per_tile - 1, 0), j)),
                  pl.BlockSpec((FFN_CONV_WIDTH, tn), lambda i, j: (0, j)),
                  pl.BlockSpec((1, tn), lambda i, j: (0, j))],
        out_specs=pl.BlockSpec((tm, tn), lambda i, j: (i, j)),
        scratch_shapes=[pltpu.VMEM((FFN_HALO + tm, tn), F32)],
        compiler_params=_cparams(("parallel", "parallel")),
        name="ffn_mid",
    )(gv, gv, gv, w_dw, b_dw.reshape(1, F))


def _pad_to(z, axis, size):
    pad = [(0, 0)] * z.ndim
    pad[axis] = (0, size - z.shape[axis])
    return jnp.pad(z, pad)


def _round_up(n, m):
    return (n + m - 1) // m * m


def _tiles(M, seq_len):
    big = min(1024, seq_len)
    return dict(
        norm=min(512, seq_len), rwkv_in=min(256, seq_len), lora=min(256, seq_len),
        gemm_m=big, gemm_n=512, gemm_m_deep=min(512, seq_len),
        conf=min(256, seq_len), ffn_m=min(512, seq_len), ffn_n=512,
        wkv_par_chunks=min(4, seq_len // WKV_CHUNK), wkv_seq_chunks=min(4, seq_len // WKV_CHUNK),
        mod_n=1024,
    )


def kernel(x, c, ada_w, ada_b, norm_mix_g, norm_ffn_g, rwkv_mu, rwkv_w_rkv, rwkv_w0, rwkv_w1, rwkv_w2, rwkv_a0, rwkv_a1, rwkv_a2, rwkv_v0, rwkv_v1, rwkv_v2, rwkv_g1, rwkv_g2, rwkv_k_k, rwkv_k_a, rwkv_r_k, rwkv_lnx_g, rwkv_lnx_b, rwkv_w_o, conv_w_pw1, conv_b_pw1, conv_w_dw, conv_b_dw, conv_ln_g, conv_ln_b, conv_w_pw2, conv_b_pw2, ffn_w_up, ffn_w_dw, ffn_b_dw, ffn_w_down, final_norm_g):
    B, T, C = x.shape
    M = B * T
    depth = ada_w.shape[0]
    F = ffn_w_dw.shape[-1]
    t = _tiles(M, T)
    Fp = _round_up(F, t["ffn_n"])
    lora_pad = lambda w, axis: _pad_to(w, axis, _round_up(w.shape[axis], LANES)).astype(BF16)

    xf = x.reshape(M, C)
    c_pad = _pad_to(c, 0, SUBLANES_F32)
    mod = _adaln_mod(c_pad, ada_w, ada_b, tn=min(t["mod_n"], 6 * C))
    mod = mod[:, :B].reshape(depth, B, 6, C)
    masks = _wkv_masks()

    v_first = None
    for i in range(depth):
        mod_l = mod[i]
        j = i // 2
        if i % 2 == 0:
            xs = _rwkv_in(xf, norm_mix_g[i], mod_l, rwkv_mu[j], rows=(0, 1), tm=t["rwkv_in"], seq_len=T)
            rkv = _gemm(xs, rwkv_w_rkv[j].astype(BF16), out_dtype=F32, tm=t["gemm_m"], tn=t["gemm_n"])
            lora_w = (lora_pad(rwkv_w1[j], 1), lora_pad(rwkv_w2[j], 0), lora_pad(rwkv_a1[j], 1),
                      lora_pad(rwkv_a2[j], 0), rwkv_g1[j].astype(BF16), rwkv_g2[j].astype(BF16),
                      rwkv_w0[j], rwkv_a0[j])
            if v_first is None:
                lw, a_sig, gate = _rwkv_lora(xs, *lora_w, tm=t["lora"])
                v = rkv[2]
                v_first = v
            else:
                vmix = (lora_pad(rwkv_v1[j - 1], 1), lora_pad(rwkv_v2[j - 1], 0), rwkv_v0[j - 1], rkv, v_first)
                lw, a_sig, gate, v = _rwkv_lora(xs, *lora_w, vmix=vmix, tm=t["lora"])
            rp, p2, gm, hm, wl, bonus = _wkv_par(rkv, v, lw, a_sig, rwkv_k_k[j], rwkv_k_a[j],
                                                 rwkv_r_k[j].reshape(C), masks, chunks=t["wkv_par_chunks"])
            y = _wkv_seq(rp, p2, gm, hm, wl, bonus, gate, rwkv_lnx_g[j], rwkv_lnx_b[j], masks[3],
                         chunks=t["wkv_seq_chunks"], seq_len=T)
            xf = _gemm(y[None], rwkv_w_o[j].astype(BF16)[None], res=xf, mod_l=mod_l, gate_row=2,
                       out_dtype=F32, tm=t["gemm_m"], tn=t["gemm_n"], seq_len=T)[0]
        else:
            h = _norm_mod(xf, norm_mix_g[i], mod_l, rows=(0, 1), out_dtype=BF16, tm=t["norm"], seq_len=T)
            p = _gemm(h[None], conv_w_pw1[j].astype(BF16)[None], bias=conv_b_pw1[j][None],
                      out_dtype=BF16, tm=t["gemm_m"], tn=t["gemm_n"])
            u = _conf_mid(p, conv_w_dw[j], conv_b_dw[j], conv_ln_g[j], conv_ln_b[j], tm=t["conf"], seq_len=T)
            xf = _gemm(u[None], conv_w_pw2[j].astype(BF16)[None], bias=conv_b_pw2[j][None], res=xf,
                       mod_l=mod_l, gate_row=2, out_dtype=F32, tm=t["gemm_m"], tn=t["gemm_n"], seq_len=T)[0]

        h = _norm_mod(xf, norm_ffn_g[i], mod_l, rows=(3, 4), out_dtype=BF16, tm=t["norm"], seq_len=T)
        w_up = ffn_w_up[i]
        w_gv = jnp.stack([_pad_to(w_up[:, :F], 1, Fp), _pad_to(w_up[:, F:], 1, Fp)]).astype(BF16)
        gv = _gemm(h[None], w_gv, out_dtype=BF16, tm=t["gemm_m"], tn=t["gemm_n"])
        z = _ffn_mid(gv, _pad_to(ffn_w_dw[i], 1, Fp), _pad_to(ffn_b_dw[i], 0, Fp),
                     tm=t["ffn_m"], tn=t["ffn_n"], seq_len=T)
        xf = _gemm(z[None], _pad_to(ffn_w_down[i], 0, Fp).astype(BF16)[None], res=xf, mod_l=mod_l, gate_row=5,
                   out_dtype=F32, tm=t["gemm_m_deep"], tn=t["gemm_n"], seq_len=T)[0]

    out = _norm_mod(xf, final_norm_g, mod[0], rows=None, out_dtype=F32, tm=t["norm"], seq_len=T)
    return out.reshape(B, T, C)
```

```python
import functools

import jax
import jax.numpy as jnp
from jax import lax
from jax.experimental import pallas as pl
from jax.experimental.pallas import tpu as pltpu

F32 = jnp.float32
BF16 = jnp.bfloat16

HEAD_SIZE = 64
RMS_EPS = 1e-6
LN_EPS = 1e-5
GN_EPS = 64e-5
CONV_WIDTH = 31
FFN_CONV_WIDTH = 3

LANES = 128
SUBLANES_F32 = 8
SUBLANES_BF16 = 16
MXU_DIM = 256
VMEM_LIMIT_BYTES = 56 * 1024 * 1024

WKV_CHUNK = HEAD_SIZE
QUAD = MXU_DIM
HEADS_PER_QUAD = QUAD // HEAD_SIZE
CONV_HALO = 32
FFN_HALO = 16


def _cparams(semantics):
    return pltpu.CompilerParams(dimension_semantics=semantics,
                                vmem_limit_bytes=VMEM_LIMIT_BYTES)


def _sigmoid(z):
    return 1.0 / (1.0 + jnp.exp(-z))


def _dot(a, b):
    return jnp.dot(a.astype(BF16), b.astype(BF16), preferred_element_type=F32)


def _dot_nt(a, b):
    return lax.dot_general(a.astype(BF16), b.astype(BF16), (((1,), (1,)), ((), ())),
                           preferred_element_type=F32)


def _dot_tn(a, b):
    return lax.dot_general(a.astype(BF16), b.astype(BF16), (((0,), (0,)), ((), ())),
                           preferred_element_type=F32)


def _split2(z):
    hi = z.astype(BF16)
    lo = (z - hi.astype(F32)).astype(BF16)
    return hi, lo


def _split3(z):
    hi = z.astype(BF16)
    r1 = z - hi.astype(F32)
    mid = r1.astype(BF16)
    lo = (r1 - mid.astype(F32)).astype(BF16)
    return hi, mid, lo


def _mod_kernel(c_ref, w_ref, b_ref, o_ref):
    c = c_ref[...]
    c_act = (c * _sigmoid(c)).astype(BF16)
    o_ref[0] = jnp.dot(c_act, w_ref[0].astype(BF16), preferred_element_type=F32) + b_ref[0]


def _adaln_mod(c_pad, ada_w, ada_b, *, tn):
    depth, C, n6 = ada_w.shape
    rows = c_pad.shape[0]
    return pl.pallas_call(
        _mod_kernel,
        out_shape=jax.ShapeDtypeStruct((depth, rows, n6), F32),
        grid=(depth, n6 // tn),
        in_specs=[pl.BlockSpec((rows, C), lambda l, j: (0, 0)),
                  pl.BlockSpec((1, C, tn), lambda l, j: (l, 0, j)),
                  pl.BlockSpec((1, 1, tn), lambda l, j: (l, 0, j))],
        out_specs=pl.BlockSpec((1, rows, tn), lambda l, j: (l, 0, j)),
        compiler_params=_cparams(("parallel", "parallel")),
        name="adaln_mod",
    )(c_pad, ada_w, ada_b.reshape(depth, 1, n6))


def _rms_mod(x, g, mod_ref, rows):
    y = x * lax.rsqrt(jnp.mean(x * x, axis=-1, keepdims=True) + RMS_EPS) * g
    if rows is not None:
        shift_row, scale_row = rows
        y = y * (1.0 + mod_ref[0, scale_row:scale_row + 1, :]) + mod_ref[0, shift_row:shift_row + 1, :]
    return y


def _norm_kernel(x_ref, g_ref, mod_ref, o_ref, *, rows):
    o_ref[...] = _rms_mod(x_ref[...], g_ref[...], mod_ref, rows).astype(o_ref.dtype)


def _norm_mod(x, g, mod_l, *, rows, out_dtype, tm, seq_len):
    M, C = x.shape
    tiles_per_seq = seq_len // tm
    return pl.pallas_call(
        functools.partial(_norm_kernel, rows=rows),
        out_shape=jax.ShapeDtypeStruct((M, C), out_dtype),
        grid=(M // tm,),
        in_specs=[pl.BlockSpec((tm, C), lambda i: (i, 0)),
                  pl.BlockSpec((1, C), lambda i: (0, 0)),
                  pl.BlockSpec((1,) + mod_l.shape[1:], lambda i: (i // tiles_per_seq, 0, 0))],
        out_specs=pl.BlockSpec((tm, C), lambda i: (i, 0)),
        compiler_params=_cparams(("parallel",)),
        name="norm_mod",
    )(x, g.reshape(1, C), mod_l)


def _rwkv_in_kernel(x_ref, xh_ref, g_ref, mod_ref, mu_ref, o_ref, *, rows, tiles_per_seq):
    i = pl.program_id(0)
    g = g_ref[...]
    h = _rms_mod(x_ref[...], g, mod_ref, rows)
    h_halo = _rms_mod(xh_ref[...], g, mod_ref, rows)
    at_seq_start = (i % tiles_per_seq) == 0
    prev_row = jnp.where(at_seq_start, 0.0, h_halo[SUBLANES_F32 - 1:SUBLANES_F32, :])
    rolled = pltpu.roll(h, 1, axis=0)
    row = lax.broadcasted_iota(jnp.int32, h.shape, 0)
    xx = jnp.where(row == 0, prev_row, rolled) - h
    for n in range(o_ref.shape[0]):
        o_ref[n] = (h + xx * mu_ref[n:n + 1, :]).astype(o_ref.dtype)


def _rwkv_in(x, g, mod_l, mu, *, rows, tm, seq_len):
    M, C = x.shape
    tiles_per_seq = seq_len // tm
    halo_blocks_per_tile = tm // SUBLANES_F32
    n_mix = mu.shape[0]
    return pl.pallas_call(
        functools.partial(_rwkv_in_kernel, rows=rows, tiles_per_seq=tiles_per_seq),
        out_shape=jax.ShapeDtypeStruct((n_mix, M, C), BF16),
        grid=(M // tm,),
        in_specs=[pl.BlockSpec((tm, C), lambda i: (i, 0)),
                  pl.BlockSpec((SUBLANES_F32, C),
                               lambda i: (jnp.maximum(i * halo_blocks_per_tile - 1, 0), 0)),
                  pl.BlockSpec((1, C), lambda i: (0, 0)),
                  pl.BlockSpec((1,) + mod_l.shape[1:], lambda i: (i // tiles_per_seq, 0, 0)),
                  pl.BlockSpec((n_mix, C), lambda i: (0, 0))],
        out_specs=pl.BlockSpec((n_mix, tm, C), lambda i: (0, i, 0)),
        compiler_params=_cparams(("parallel",)),
        name="rwkv_in",
    )(x, x, g.reshape(1, C), mod_l, mu)


def _gemm_kernel(*refs, has_bias, gate_row):
    x_ref, w_ref = refs[0], refs[1]
    pos = 2
    acc = jnp.dot(x_ref[0], w_ref[0], preferred_element_type=F32)
    if has_bias:
        acc = acc + refs[pos][0]
        pos += 1
    if gate_row is not None:
        res_ref, mod_ref = refs[pos], refs[pos + 1]
        pos += 2
        acc = res_ref[...] + mod_ref[0, gate_row:gate_row + 1, :] * acc
    o_ref = refs[pos]
    o_ref[0] = acc.astype(o_ref.dtype)


def _gemm(x, w, *, bias=None, res=None, mod_l=None, gate_row=None, out_dtype, tm, tn, seq_len=None):
    nx, M, K = x.shape
    nb, _, N = w.shape
    in_specs = [pl.BlockSpec((1, tm, K), (lambda n, i, j: (n, i, 0)) if nx > 1 else (lambda n, i, j: (0, i, 0))),
                pl.BlockSpec((1, K, tn), lambda n, i, j: (n, 0, j))]
    args = [x, w]
    if bias is not None:
        in_specs.append(pl.BlockSpec((1, 1, tn), lambda n, i, j: (n, 0, j)))
        args.append(bias.reshape(nb, 1, N).astype(F32))
    if gate_row is not None:
        tiles_per_seq = seq_len // tm
        in_specs.append(pl.BlockSpec((tm, tn), lambda n, i, j: (i, j)))
        in_specs.append(pl.BlockSpec((1, mod_l.shape[1], tn), lambda n, i, j: (i // tiles_per_seq, 0, j)))
        args += [res, mod_l]
    return pl.pallas_call(
        functools.partial(_gemm_kernel, has_bias=bias is not None, gate_row=gate_row),
        out_shape=jax.ShapeDtypeStruct((nb, M, N), out_dtype),
        grid=(nb, M // tm, N // tn),
        in_specs=in_specs,
        out_specs=pl.BlockSpec((1, tm, tn), lambda n, i, j: (n, i, j)),
        compiler_params=_cparams(("parallel", "parallel", "parallel")),
        name="gemm",
    )(*args)


def _lora_kernel(*refs, has_v):
    (xw_ref, xa_ref, xg_ref, w1_ref, w2_ref, a1_ref, a2_ref, g1_ref, g2_ref, w0_ref, a0_ref) = refs[:11]
    pos = 11
    if has_v:
        xv_ref, v1_ref, v2_ref, v0_ref, v_ref, vf_ref = refs[pos:pos + 6]
        pos += 6
    lw_ref, a_ref, g_ref = refs[pos:pos + 3]
    pos += 3

    zw = w0_ref[...] + _dot(jnp.tanh(_dot(xw_ref[0], w1_ref[...])), w2_ref[...])
    neg = -zw
    softplus = jnp.maximum(neg, 0.0) + jnp.log(1.0 + jnp.exp(-jnp.abs(neg)))
    lw_ref[...] = -jnp.exp(-softplus - 0.5)
    a_ref[...] = _sigmoid(a0_ref[...] + _dot(_dot(xa_ref[0], a1_ref[...]), a2_ref[...]))
    g_ref[...] = _dot(_sigmoid(_dot(xg_ref[0], g1_ref[...])), g2_ref[...]).astype(g_ref.dtype)
    if has_v:
        vo_ref = refs[pos]
        mix = _sigmoid(v0_ref[...] + _dot(_dot(xv_ref[0], v1_ref[...]), v2_ref[...]))
        v = v_ref[0]
        vo_ref[...] = v + (vf_ref[...] - v) * mix


def _rwkv_lora(xs, w1, w2, a1, a2, g1, g2, w0, a0, *, vmix=None, tm):
    _, M, C = xs.shape
    row = lambda n: pl.BlockSpec((1, tm, C), lambda i, n=n: (n, i, 0))
    full = lambda arr: pl.BlockSpec(arr.shape, lambda i: (0,) * arr.ndim)
    tile = pl.BlockSpec((tm, C), lambda i: (i, 0))
    vec = lambda z: z.reshape(1, C)
    args = [xs, xs, xs, w1, w2, a1, a2, g1, g2, vec(w0), vec(a0)]
    in_specs = [row(3), row(4), row(5)] + [full(z) for z in args[3:]]
    out_shape = [jax.ShapeDtypeStruct((M, C), F32), jax.ShapeDtypeStruct((M, C), F32),
                 jax.ShapeDtypeStruct((M, C), BF16)]
    out_specs = [tile, tile, tile]
    if vmix is not None:
        v1, v2, v0, rkv, v_first = vmix
        extra = [xs, v1, v2, vec(v0), rkv, v_first]
        args += extra
        in_specs += [row(2), full(v1), full(v2), full(extra[3]),
                     pl.BlockSpec((1, tm, C), lambda i: (2, i, 0)), tile]
        out_shape.append(jax.ShapeDtypeStruct((M, C), F32))
        out_specs.append(tile)
    return pl.pallas_call(
        functools.partial(_lora_kernel, has_v=vmix is not None),
        out_shape=out_shape,
        grid=(M // tm,),
        in_specs=in_specs,
        out_specs=out_specs,
        compiler_params=_cparams(("parallel",)),
        name="rwkv_lora",
    )(*args)


def _wkv_masks():
    L, Q = WKV_CHUNK, QUAD
    idx = jnp.arange(Q)
    t = jnp.arange(L)[:, None]
    s = (idx % L)[None, :]
    tri = (t >= jnp.arange(L)[None, :]).astype(BF16)
    strict = (t > s).astype(F32)
    incl = (t >= s).astype(F32)
    block = ((idx[:, None] // L) == (idx[None, :] // L)).astype(BF16)
    return tri, strict, incl, block


def _row_block_sum(z):
    L = WKV_CHUNK
    out = z[0:L]
    for h in range(1, HEADS_PER_QUAD):
        out = out + z[h * L:(h + 1) * L]
    return out


def _block_diag(z, block):
    return jnp.concatenate([z.astype(BF16)] * HEADS_PER_QUAD, axis=0) * block


def _head_sums(zs, block):
    n = zs[0].shape[0]
    parts = []
    for z in zs:
        parts += list(_split2(z))
    out = jnp.dot(jnp.concatenate(parts, axis=0), block, preferred_element_type=F32)
    return [out[2 * i * n:(2 * i + 1) * n] + out[(2 * i + 1) * n:(2 * i + 2) * n] for i in range(len(zs))]


def _wkv_par_kernel(r_ref, k_ref, v_ref, lw_ref, a_ref, kk_w_ref, ka_w_ref, rk_w_ref,
                    tri_ref, strict_ref, incl_ref, block_ref,
                    rp_ref, p2_ref, g_out_ref, h_out_ref, wl_ref, bonus_ref, *, chunks):
    L = WKV_CHUNK
    tri = tri_ref[...]
    block = block_ref[...]
    block_f32 = block.astype(F32)
    strict = strict_ref[...] > 0.5
    incl = incl_ref[...] > 0.5
    eye = incl_ref[...] - strict_ref[...]

    Q = QUAD
    each = range(chunks)
    rows = [slice(c * L, (c + 1) * L) for c in each]
    r = [r_ref[0, rw, :] for rw in rows]
    k = [k_ref[0, rw, :] for rw in rows]
    v = [v_ref[rw, :] for rw in rows]
    lw = [lw_ref[rw, :] for rw in rows]
    a_sig = [a_ref[rw, :] for rw in rows]

    parts = []
    for c in each:
        parts += list(_split3(lw[c]))
    sums = jnp.dot(tri, jnp.concatenate(parts, axis=1), preferred_element_type=F32)
    cum = [sums[:, 3 * c * Q:(3 * c + 1) * Q] + sums[:, (3 * c + 1) * Q:(3 * c + 2) * Q]
           + sums[:, (3 * c + 2) * Q:(3 * c + 3) * Q] for c in each]

    k_raw = [k[c] * kk_w_ref[...] for c in each]
    k_mod = [k[c] * (1.0 + (a_sig[c] - 1.0) * ka_w_ref[...]) for c in each]
    head = _head_sums([k_raw[c] * k_raw[c] for c in each]
                      + [r[c] * k_mod[c] * rk_w_ref[...] for c in each], block)
    a_t, r_t, b_end, k_end, n_ab, n_ak, n_rb, n_rk = [], [], [], [], [], [], [], []
    for c in each:
        cum_end = cum[c][L - 1:L, :]
        w_inv = jnp.exp(-cum[c])
        w_to_end = jnp.exp(cum_end - cum[c])
        kk = k_raw[c] / jnp.maximum(jnp.sqrt(head[c]), 1e-12)
        b = kk * a_sig[c]
        a_t.append(-kk * jnp.exp(cum[c] - lw[c]))
        r_t.append(r[c] * jnp.exp(cum[c]))
        b_end.append(b * w_to_end)
        k_end.append(k_mod[c] * w_to_end)
        wl_ref[c] = jnp.exp(cum_end)
        bonus_ref[rows[c], :] = head[chunks + c] * v[c]
        rhs = jnp.concatenate([_block_diag(b * w_inv, block), _block_diag(k_mod[c] * w_inv, block)], axis=0)
        by = _dot_nt(jnp.concatenate([a_t[c], r_t[c]], axis=0), rhs)
        n_ab.append(jnp.where(strict, by[:L, :Q], 0.0))
        n_ak.append(jnp.where(strict, by[:L, Q:], 0.0))
        n_rb.append(jnp.where(incl, by[L:, :Q], 0.0))
        n_rk.append(jnp.where(incl, by[L:, Q:], 0.0))

    v_block = [_block_diag(v[c], block) for c in each]
    p1 = [_dot(n_ak[c], v_block[c]) for c in each]
    p2_v = [_dot(n_rk[c], v_block[c]) for c in each]

    power = list(n_ab)
    t_inv = [eye + n_ab[c] for c in each]
    span = 2
    while span < L:
        for c in each:
            power_block = _block_diag(power[c], block)
            if span == 2:
                power[c] = _dot(power[c], power_block)
            else:
                both = _dot(jnp.concatenate([power[c], t_inv[c]], axis=0), power_block)
                power[c] = both[:L]
                t_inv[c] = t_inv[c] + both[L:]
        span *= 2
    for c in each:
        t_inv[c] = t_inv[c] + _dot(t_inv[c], _block_diag(power[c], block))

    solved = [_dot(t_inv[c], jnp.concatenate([_block_diag(a_t[c], block), _block_diag(p1[c], block)], axis=1))
              for c in each]
    a_p = [z[:, :Q] for z in solved]
    p1_p = [z[:, Q:] for z in solved]
    read = [_dot(n_rb[c], jnp.concatenate([_block_diag(a_p[c], block), _block_diag(p1_p[c], block)], axis=1))
            for c in each]
    for c in each:
        rp_ref[rows[c], :] = (r_t[c] + read[c][:, :Q]).astype(rp_ref.dtype)
        p2_ref[rows[c], :] = read[c][:, Q:] + p2_v[c]
    for c in each:
        g_map = _row_block_sum(_dot_tn(a_p[c], b_end[c]) * block_f32)
        h_map = _row_block_sum(_dot_tn(jnp.concatenate([p1_p[c], v[c]], axis=0),
                                       jnp.concatenate([b_end[c], k_end[c]], axis=0)) * block_f32)
        g_out_ref[rows[c], :] = g_map.astype(g_out_ref.dtype)
        h_out_ref[rows[c], :] = h_map


def _wkv_par(rkv, v, lw, a_sig, k_k, k_a, r_k, masks, *, chunks):
    _, M, C = rkv.shape
    L, Q = WKV_CHUNK, QUAD
    rows = chunks * L
    tile = pl.BlockSpec((rows, Q), lambda i, q: (i, q))
    rk_tile = lambda n: pl.BlockSpec((1, rows, Q), lambda i, q, n=n: (n, i, q))
    vec = pl.BlockSpec((1, Q), lambda i, q: (0, q))
    const = lambda arr: pl.BlockSpec(arr.shape, lambda i, q: (0,) * arr.ndim)
    out_shape = [jax.ShapeDtypeStruct((M, C), BF16), jax.ShapeDtypeStruct((M, C), F32),
                 jax.ShapeDtypeStruct((M, C), BF16), jax.ShapeDtypeStruct((M, C), F32),
                 jax.ShapeDtypeStruct((M // L, 1, C), F32), jax.ShapeDtypeStruct((M, C), F32)]
    out_specs = [tile, tile, tile, tile, pl.BlockSpec((chunks, 1, Q), lambda i, q: (i, 0, q)), tile]
    return pl.pallas_call(
        functools.partial(_wkv_par_kernel, chunks=chunks),
        out_shape=out_shape,
        grid=(M // rows, C // Q),
        in_specs=[rk_tile(0), rk_tile(1), tile, tile, tile, vec, vec, vec] + [const(m) for m in masks],
        out_specs=out_specs,
        compiler_params=_cparams(("parallel", "parallel")),
        name="wkv_par",
    )(rkv, rkv, v, lw, a_sig, k_k.reshape(1, C), k_a.reshape(1, C), r_k.reshape(1, C), *masks)


def _wkv_seq_kernel(rp_ref, p2_ref, gm_ref, hm_ref, wl_ref, bonus_ref, gate_ref, lng_ref, lnb_ref,
                    block_ref, o_ref, s_ref, *, chunks, quads):
    L, Q = WKV_CHUNK, QUAD

    @pl.when(pl.program_id(1) == 0)
    def _():
        s_ref[...] = jnp.zeros_like(s_ref)

    block = block_ref[...]
    inv_n = 1.0 / HEAD_SIZE

    def chunk_body(c, carry):
        rows = pl.ds(pl.multiple_of(c * L, L), L)
        for q in range(quads):
            lanes = slice(q * Q, (q + 1) * Q)
            s = s_ref[q]
            y = _dot_nt(rp_ref[rows, lanes], _block_diag(s, block)) + p2_ref[rows, lanes]
            s_ref[q] = (s * wl_ref[c, :, lanes] + _dot(s, _block_diag(gm_ref[rows, lanes], block))
                        + hm_ref[rows, lanes])

            mean = _head_sums([y], block)[0] * inv_n
            d = y - mean
            var = _head_sums([d * d], block)[0] * inv_n
            yn = d * lax.rsqrt(var + GN_EPS) * lng_ref[:, lanes] + lnb_ref[:, lanes]
            out = (yn + bonus_ref[rows, lanes]) * gate_ref[rows, lanes].astype(F32)
            o_ref[rows, lanes] = out.astype(o_ref.dtype)
        return carry

    lax.fori_loop(0, chunks, chunk_body, 0)


def _wkv_seq(rp, p2, gm, hm, wl, bonus, gate, lnx_g, lnx_b, block, *, chunks, seq_len):
    M, C = rp.shape
    L, Q = WKV_CHUNK, QUAD
    rows = chunks * L
    steps_per_seq = seq_len // rows
    quads = C // Q
    tile = pl.BlockSpec((rows, C), lambda b, i: (b * steps_per_seq + i, 0))
    vec = pl.BlockSpec((1, C), lambda b, i: (0, 0))
    return pl.pallas_call(
        functools.partial(_wkv_seq_kernel, chunks=chunks, quads=quads),
        out_shape=jax.ShapeDtypeStruct((M, C), BF16),
        grid=(M // seq_len, steps_per_seq),
        in_specs=[tile, tile, tile, tile,
                  pl.BlockSpec((chunks, 1, C), lambda b, i: (b * steps_per_seq + i, 0, 0)),
                  tile, tile, vec, vec,
                  pl.BlockSpec(block.shape, lambda b, i: (0, 0))],
        out_specs=tile,
        scratch_shapes=[pltpu.VMEM((quads, L, Q), F32)],
        compiler_params=_cparams(("parallel", "arbitrary")),
        name="wkv_seq",
    )(rp, p2, gm, hm, wl, bonus, gate, lnx_g.reshape(1, C), lnx_b.reshape(1, C), block)


def _conf_mid_kernel(a_ref, b_ref, ah_ref, bh_ref, wdw_ref, bdw_ref, lng_ref, lnb_ref, o_ref,
                     u_scr, c_scr, *, tiles_per_seq):
    tm, C = o_ref.shape
    at_seq_start = (pl.program_id(0) % tiles_per_seq) == 0
    u_halo = ah_ref[0].astype(F32) * _sigmoid(bh_ref[0].astype(F32))
    u_scr[0:CONV_HALO, :] = jnp.where(at_seq_start, 0.0, u_halo)
    u_scr[CONV_HALO:CONV_HALO + tm, :] = a_ref[0].astype(F32) * _sigmoid(b_ref[0].astype(F32))
    first_tap = CONV_HALO - (CONV_WIDTH - 1)

    def slab(s, carry):
        lanes = pl.ds(pl.multiple_of(s * LANES, LANES), LANES)
        acc = jnp.zeros((tm, LANES), F32) + bdw_ref[:, lanes]
        for tap in range(CONV_WIDTH):
            acc = acc + u_scr[pl.ds(first_tap + tap, tm), lanes] * wdw_ref[tap:tap + 1, lanes]
        c_scr[:, lanes] = acc
        return carry

    lax.fori_loop(0, C // LANES, slab, 0)
    y = c_scr[...]
    mu = jnp.mean(y, axis=-1, keepdims=True)
    d = y - mu
    var = jnp.mean(d * d, axis=-1, keepdims=True)
    yn = d * lax.rsqrt(var + LN_EPS) * lng_ref[...] + lnb_ref[...]
    o_ref[...] = (yn * _sigmoid(yn)).astype(o_ref.dtype)


def _conf_mid(p, w_dw, b_dw, ln_g, ln_b, *, tm, seq_len):
    _, M, C2 = p.shape
    C = C2 // 2
    tiles_per_seq = seq_len // tm
    halo_per_tile = tm // CONV_HALO
    halo_idx = lambda i: jnp.maximum(i * halo_per_tile - 1, 0)
    vec = pl.BlockSpec((1, C), lambda i: (0, 0))
    return pl.pallas_call(
        functools.partial(_conf_mid_kernel, tiles_per_seq=tiles_per_seq),
        out_shape=jax.ShapeDtypeStruct((M, C), BF16),
        grid=(M // tm,),
        in_specs=[pl.BlockSpec((1, tm, C), lambda i: (0, i, 0)),
                  pl.BlockSpec((1, tm, C), lambda i: (0, i, 1)),
                  pl.BlockSpec((1, CONV_HALO, C), lambda i: (0, halo_idx(i), 0)),
                  pl.BlockSpec((1, CONV_HALO, C), lambda i: (0, halo_idx(i), 1)),
                  pl.BlockSpec(w_dw.shape, lambda i: (0, 0)), vec, vec, vec],
        out_specs=pl.BlockSpec((tm, C), lambda i: (i, 0)),
        scratch_shapes=[pltpu.VMEM((CONV_HALO + tm, C), F32), pltpu.VMEM((tm, C), F32)],
        compiler_params=_cparams(("parallel",)),
        name="conformer_mid",
    )(p, p, p, p, w_dw, b_dw.reshape(1, C), ln_g.reshape(1, C), ln_b.reshape(1, C))


def _ffn_up_kernel(x_ref, xh_ref, g_ref, mod_ref, wg_ref, wv_ref, wdw_ref, bdw_ref, o_ref,
                   h_scr, gate_scr, *, rows, tiles_per_seq):
    tm, tn = o_ref.shape

    @pl.when(pl.program_id(1) == 0)
    def _():
        g = g_ref[...]
        at_seq_start = (pl.program_id(0) % tiles_per_seq) == 0
        h_halo = _rms_mod(xh_ref[...], g, mod_ref, rows)
        h_scr[0:FFN_HALO, :] = jnp.where(at_seq_start, 0.0, h_halo).astype(h_scr.dtype)
        h_scr[FFN_HALO:FFN_HALO + tm, :] = _rms_mod(x_ref[...], g, mod_ref, rows).astype(h_scr.dtype)

    gate_scr[...] = jnp.dot(h_scr[...], wg_ref[0], preferred_element_type=F32)
    val = jnp.dot(h_scr[FFN_HALO:FFN_HALO + tm, :], wv_ref[0], preferred_element_type=F32)
    first_tap = FFN_HALO - (FFN_CONV_WIDTH - 1)
    acc = jnp.zeros((tm, tn), F32) + bdw_ref[...]
    for tap in range(FFN_CONV_WIDTH):
        acc = acc + gate_scr[pl.ds(first_tap + tap, tm), :] * wdw_ref[tap:tap + 1, :]
    o_ref[...] = (acc * _sigmoid(acc) * val).astype(o_ref.dtype)


def _ffn_up(x, g, mod_l, w_gv, w_dw, b_dw, *, rows, tm, tn, seq_len):
    M, C = x.shape
    F = w_gv.shape[-1]
    tiles_per_seq = seq_len // tm
    halo_per_tile = tm // FFN_HALO
    return pl.pallas_call(
        functools.partial(_ffn_up_kernel, rows=rows, tiles_per_seq=tiles_per_seq),
        out_shape=jax.ShapeDtypeStruct((M, F), BF16),
        grid=(M // tm, F // tn),
        in_specs=[pl.BlockSpec((tm, C), lambda i, j: (i, 0)),
                  pl.BlockSpec((FFN_HALO, C), lambda i, j: (jnp.maximum(i * halo_per_tile - 1, 0), 0)),
                  pl.BlockSpec((1, C), lambda i, j: (0, 0)),
                  pl.BlockSpec((1,) + mod_l.shape[1:], lambda i, j: (i // tiles_per_seq, 0, 0)),
                  pl.BlockSpec((1, C, tn), lambda i, j: (0, 0, j)),
                  pl.BlockSpec((1, C, tn), lambda i, j: (1, 0, j)),
                  pl.BlockSpec((FFN_CONV_WIDTH, tn), lambda i, j: (0, j)),
                  pl.BlockSpec((1, tn), lambda i, j: (0, j))],
        out_specs=pl.BlockSpec((tm, tn), lambda i, j: (i, j)),
        scratch_shapes=[pltpu.VMEM((FFN_HALO + tm, C), BF16), pltpu.VMEM((FFN_HALO + tm, tn), F32)],
        compiler_params=_cparams(("parallel", "arbitrary")),
        name="ffn_up",
    )(x, x, g.reshape(1, C), mod_l, w_gv, w_gv, w_dw, b_dw.reshape(1, F))


def _pad_to(z, axis, size):
    pad = [(0, 0)] * z.ndim
    pad[axis] = (0, size - z.shape[axis])
    return jnp.pad(z, pad)


def _round_up(n, m):
    return (n + m - 1) // m * m


def _tiles(M, seq_len):
    big = min(1024, seq_len)
    return dict(
        norm=min(512, seq_len), rwkv_in=min(256, seq_len), lora=min(256, seq_len),
        gemm_m=big, gemm_n=512, gemm_m_deep=min(512, seq_len),
        conf=min(256, seq_len), ffn_m=big, ffn_n=512,
        wkv_par_chunks=min(8, seq_len // WKV_CHUNK), wkv_seq_chunks=min(4, seq_len // WKV_CHUNK),
        mod_n=1024,
    )


def kernel(x, c, ada_w, ada_b, norm_mix_g, norm_ffn_g, rwkv_mu, rwkv_w_rkv, rwkv_w0, rwkv_w1, rwkv_w2, rwkv_a0, rwkv_a1, rwkv_a2, rwkv_v0, rwkv_v1, rwkv_v2, rwkv_g1, rwkv_g2, rwkv_k_k, rwkv_k_a, rwkv_r_k, rwkv_lnx_g, rwkv_lnx_b, rwkv_w_o, conv_w_pw1, conv_b_pw1, conv_w_dw, conv_b_dw, conv_ln_g, conv_ln_b, conv_w_pw2, conv_b_pw2, ffn_w_up, ffn_w_dw, ffn_b_dw, ffn_w_down, final_norm_g):
    B, T, C = x.shape
    M = B * T
    depth = ada_w.shape[0]
    F = ffn_w_dw.shape[-1]
    t = _tiles(M, T)
    Fp = _round_up(F, t["ffn_n"])
    lora_pad = lambda w, axis: _pad_to(w, axis, _round_up(w.shape[axis], LANES)).astype(BF16)

    xf = x.reshape(M, C)
    c_pad = _pad_to(c, 0, SUBLANES_F32)
    mod = _adaln_mod(c_pad, ada_w, ada_b, tn=min(t["mod_n"], 6 * C))
    mod = mod[:, :B].reshape(depth, B, 6, C)
    masks = _wkv_masks()

    v_first = None
    for i in range(depth):
        mod_l = mod[i]
        j = i // 2
        if i % 2 == 0:
            xs = _rwkv_in(xf, norm_mix_g[i], mod_l, rwkv_mu[j], rows=(0, 1), tm=t["rwkv_in"], seq_len=T)
            rkv = _gemm(xs, rwkv_w_rkv[j].astype(BF16), out_dtype=F32, tm=t["gemm_m"], tn=t["gemm_n"])
            lora_w = (lora_pad(rwkv_w1[j], 1), lora_pad(rwkv_w2[j], 0), lora_pad(rwkv_a1[j], 1),
                      lora_pad(rwkv_a2[j], 0), rwkv_g1[j].astype(BF16), rwkv_g2[j].astype(BF16),
                      rwkv_w0[j], rwkv_a0[j])
            if v_first is None:
                lw, a_sig, gate = _rwkv_lora(xs, *lora_w, tm=t["lora"])
                v = rkv[2]
                v_first = v
            else:
                vmix = (lora_pad(rwkv_v1[j - 1], 1), lora_pad(rwkv_v2[j - 1], 0), rwkv_v0[j - 1], rkv, v_first)
                lw, a_sig, gate, v = _rwkv_lora(xs, *lora_w, vmix=vmix, tm=t["lora"])
            rp, p2, gm, hm, wl, bonus = _wkv_par(rkv, v, lw, a_sig, rwkv_k_k[j], rwkv_k_a[j],
                                                 rwkv_r_k[j].reshape(C), masks, chunks=t["wkv_par_chunks"])
            y = _wkv_seq(rp, p2, gm, hm, wl, bonus, gate, rwkv_lnx_g[j], rwkv_lnx_b[j], masks[3],
                         chunks=t["wkv_seq_chunks"], seq_len=T)
            xf = _gemm(y[None], rwkv_w_o[j].astype(BF16)[None], res=xf, mod_l=mod_l, gate_row=2,
                       out_dtype=F32, tm=t["gemm_m"], tn=t["gemm_n"], seq_len=T)[0]
        else:
            h = _norm_mod(xf, norm_mix_g[i], mod_l, rows=(0, 1), out_dtype=BF16, tm=t["norm"], seq_len=T)
            p = _gemm(h[None], conv_w_pw1[j].astype(BF16)[None], bias=conv_b_pw1[j][None],
                      out_dtype=BF16, tm=t["gemm_m"], tn=t["gemm_n"])
            u = _conf_mid(p, conv_w_dw[j], conv_b_dw[j], conv_ln_g[j], conv_ln_b[j], tm=t["conf"], seq_len=T)
            xf = _gemm(u[None], conv_w_pw2[j].astype(BF16)[None], bias=conv_b_pw2[j][None], res=xf,
                       mod_l=mod_l, gate_row=2, out_dtype=F32, tm=t["gemm_m"], tn=t["gemm_n"], seq_len=T)[0]

        w_up = ffn_w_up[i]
        w_gv = jnp.stack([_pad_to(w_up[:, :F], 1, Fp), _pad_to(w_up[:, F:], 1, Fp)]).astype(BF16)
        z = _ffn_up(xf, norm_ffn_g[i], mod_l, w_gv, _pad_to(ffn_w_dw[i], 1, Fp), _pad_to(ffn_b_dw[i], 0, Fp),
                    rows=(3, 4), tm=t["ffn_m"], tn=t["ffn_n"], seq_len=T)
        xf = _gemm(z[None], _pad_to(ffn_w_down[i], 0, Fp).astype(BF16)[None], res=xf, mod_l=mod_l, gate_row=5,
                   out_dtype=F32, tm=t["gemm_m_deep"], tn=t["gemm_n"], seq_len=T)[0]

    out = _norm_mod(xf, final_norm_g, mod[0], rows=None, out_dtype=F32, tm=t["norm"], seq_len=T)
    return out.reshape(B, T, C)
```

```python
import functools
import math

import jax
import jax.numpy as jnp
from jax import lax
from jax.experimental import pallas as pl
from jax.experimental.pallas import tpu as pltpu

F32 = jnp.float32
BF16 = jnp.bfloat16

HEAD_SIZE = 64
RMS_EPS = 1e-6
LN_EPS = 1e-5
GN_EPS = 64e-5
DECAY_SCALE = -math.exp(-0.5)
CONV_WIDTH = 31
FFN_CONV_WIDTH = 3

LANES = 128
SUBLANES_F32 = 8
SUBLANES_BF16 = 16
MXU_DIM = 256
VMEM_LIMIT_BYTES = 56 * 1024 * 1024

WKV_CHUNK = HEAD_SIZE
QUAD = MXU_DIM
HEADS_PER_QUAD = QUAD // HEAD_SIZE
CONV_HALO = 32
FFN_HALO = 16


def _cparams(semantics):
    return pltpu.CompilerParams(dimension_semantics=semantics,
                                vmem_limit_bytes=VMEM_LIMIT_BYTES)


def _sigmoid(z):
    return 0.5 * jnp.tanh(0.5 * z) + 0.5


def _silu(z):
    h = 0.5 * z
    return h * jnp.tanh(h) + h


def _dot(a, b):
    return jnp.dot(a.astype(BF16), b.astype(BF16), preferred_element_type=F32)


def _dot_nt(a, b):
    return lax.dot_general(a.astype(BF16), b.astype(BF16), (((1,), (1,)), ((), ())),
                           preferred_element_type=F32)


def _dot_tn(a, b):
    return lax.dot_general(a.astype(BF16), b.astype(BF16), (((0,), (0,)), ((), ())),
                           preferred_element_type=F32)


def _split2(z):
    hi = z.astype(BF16)
    lo = (z - hi.astype(F32)).astype(BF16)
    return hi, lo


def _split3(z):
    hi = z.astype(BF16)
    r1 = z - hi.astype(F32)
    mid = r1.astype(BF16)
    lo = (r1 - mid.astype(F32)).astype(BF16)
    return hi, mid, lo


def _mod_kernel(c_ref, w_ref, b_ref, o_ref):
    c = c_ref[...]
    c_act = _silu(c).astype(BF16)
    o_ref[0] = jnp.dot(c_act, w_ref[0].astype(BF16), preferred_element_type=F32) + b_ref[0]


def _adaln_mod(c_pad, ada_w, ada_b, *, tn):
    depth, C, n6 = ada_w.shape
    rows = c_pad.shape[0]
    return pl.pallas_call(
        _mod_kernel,
        out_shape=jax.ShapeDtypeStruct((depth, rows, n6), F32),
        grid=(depth, n6 // tn),
        in_specs=[pl.BlockSpec((rows, C), lambda l, j: (0, 0)),
                  pl.BlockSpec((1, C, tn), lambda l, j: (l, 0, j)),
                  pl.BlockSpec((1, 1, tn), lambda l, j: (l, 0, j))],
        out_specs=pl.BlockSpec((1, rows, tn), lambda l, j: (l, 0, j)),
        compiler_params=_cparams(("parallel", "parallel")),
        name="adaln_mod",
    )(c_pad, ada_w, ada_b.reshape(depth, 1, n6))


def _rms_mod(x, g, mod_ref, rows):
    y = x * lax.rsqrt(jnp.mean(x * x, axis=-1, keepdims=True) + RMS_EPS) * g
    if rows is not None:
        shift_row, scale_row = rows
        y = y * (1.0 + mod_ref[0, scale_row:scale_row + 1, :]) + mod_ref[0, shift_row:shift_row + 1, :]
    return y


def _norm_kernel(x_ref, g_ref, mod_ref, o_ref, *, rows):
    o_ref[...] = _rms_mod(x_ref[...], g_ref[...], mod_ref, rows).astype(o_ref.dtype)


def _norm_mod(x, g, mod_l, *, rows, out_dtype, tm, seq_len):
    M, C = x.shape
    tiles_per_seq = seq_len // tm
    return pl.pallas_call(
        functools.partial(_norm_kernel, rows=rows),
        out_shape=jax.ShapeDtypeStruct((M, C), out_dtype),
        grid=(M // tm,),
        in_specs=[pl.BlockSpec((tm, C), lambda i: (i, 0)),
                  pl.BlockSpec((1, C), lambda i: (0, 0)),
                  pl.BlockSpec((1,) + mod_l.shape[1:], lambda i: (i // tiles_per_seq, 0, 0))],
        out_specs=pl.BlockSpec((tm, C), lambda i: (i, 0)),
        compiler_params=_cparams(("parallel",)),
        name="norm_mod",
    )(x, g.reshape(1, C), mod_l)


def _rwkv_in_kernel(x_ref, xh_ref, g_ref, mod_ref, mu_ref, o_ref, *, rows, tiles_per_seq):
    i = pl.program_id(0)
    g = g_ref[...]
    h = _rms_mod(x_ref[...], g, mod_ref, rows)
    h_halo = _rms_mod(xh_ref[...], g, mod_ref, rows)
    at_seq_start = (i % tiles_per_seq) == 0
    prev_row = jnp.where(at_seq_start, 0.0, h_halo[SUBLANES_F32 - 1:SUBLANES_F32, :])
    rolled = pltpu.roll(h, 1, axis=0)
    row = lax.broadcasted_iota(jnp.int32, h.shape, 0)
    xx = jnp.where(row == 0, prev_row, rolled) - h
    for n in range(o_ref.shape[0]):
        o_ref[n] = (h + xx * mu_ref[n:n + 1, :]).astype(o_ref.dtype)


def _rwkv_in(x, g, mod_l, mu, *, rows, tm, seq_len):
    M, C = x.shape
    tiles_per_seq = seq_len // tm
    halo_blocks_per_tile = tm // SUBLANES_F32
    n_mix = mu.shape[0]
    return pl.pallas_call(
        functools.partial(_rwkv_in_kernel, rows=rows, tiles_per_seq=tiles_per_seq),
        out_shape=jax.ShapeDtypeStruct((n_mix, M, C), BF16),
        grid=(M // tm,),
        in_specs=[pl.BlockSpec((tm, C), lambda i: (i, 0)),
                  pl.BlockSpec((SUBLANES_F32, C),
                               lambda i: (jnp.maximum(i * halo_blocks_per_tile - 1, 0), 0)),
                  pl.BlockSpec((1, C), lambda i: (0, 0)),
                  pl.BlockSpec((1,) + mod_l.shape[1:], lambda i: (i // tiles_per_seq, 0, 0)),
                  pl.BlockSpec((n_mix, C), lambda i: (0, 0))],
        out_specs=pl.BlockSpec((n_mix, tm, C), lambda i: (0, i, 0)),
        compiler_params=_cparams(("parallel",)),
        name="rwkv_in",
    )(x, x, g.reshape(1, C), mod_l, mu)


def _gemm_kernel(*refs, has_bias, gate_row, norm_rows):
    x_ref, w_ref = refs[0], refs[1]
    pos = 2
    if norm_rows is not None:
        g_ref, modn_ref, h_scr = refs[2], refs[3], refs[-1]
        pos = 4

        @pl.when(pl.program_id(2) == 0)
        def _():
            h_scr[...] = _rms_mod(x_ref[0], g_ref[...], modn_ref, norm_rows).astype(h_scr.dtype)

        lhs = h_scr[...]
    else:
        lhs = x_ref[0]
    acc = jnp.dot(lhs, w_ref[0], preferred_element_type=F32)
    if has_bias:
        acc = acc + refs[pos][0]
        pos += 1
    if gate_row is not None:
        res_ref, mod_ref = refs[pos], refs[pos + 1]
        pos += 2
        acc = res_ref[...] + mod_ref[0, gate_row:gate_row + 1, :] * acc
    o_ref = refs[pos]
    o_ref[0] = acc.astype(o_ref.dtype)


def _gemm(x, w, *, bias=None, res=None, mod_l=None, gate_row=None, norm=None, out_dtype, tm, tn, seq_len=None):
    nx, M, K = x.shape
    nb, _, N = w.shape
    tiles_per_seq = None if seq_len is None else seq_len // tm
    in_specs = [pl.BlockSpec((1, tm, K), (lambda n, i, j: (n, i, 0)) if nx > 1 else (lambda n, i, j: (0, i, 0))),
                pl.BlockSpec((1, K, tn), lambda n, i, j: (n, 0, j))]
    args = [x, w]
    scratch = []
    if norm is not None:
        g, norm_rows = norm
        in_specs.append(pl.BlockSpec((1, K), lambda n, i, j: (0, 0)))
        in_specs.append(pl.BlockSpec((1,) + mod_l.shape[1:], lambda n, i, j: (i // tiles_per_seq, 0, 0)))
        args += [g.reshape(1, K), mod_l]
        scratch.append(pltpu.VMEM((tm, K), BF16))
    if bias is not None:
        in_specs.append(pl.BlockSpec((1, 1, tn), lambda n, i, j: (n, 0, j)))
        args.append(bias.reshape(nb, 1, N).astype(F32))
    if gate_row is not None:
        in_specs.append(pl.BlockSpec((tm, tn), lambda n, i, j: (i, j)))
        in_specs.append(pl.BlockSpec((1, mod_l.shape[1], tn), lambda n, i, j: (i // tiles_per_seq, 0, j)))
        args += [res, mod_l]
    return pl.pallas_call(
        functools.partial(_gemm_kernel, has_bias=bias is not None, gate_row=gate_row,
                          norm_rows=None if norm is None else norm[1]),
        out_shape=jax.ShapeDtypeStruct((nb, M, N), out_dtype),
        grid=(nb, M // tm, N // tn),
        in_specs=in_specs,
        out_specs=pl.BlockSpec((1, tm, tn), lambda n, i, j: (n, i, j)),
        scratch_shapes=scratch,
        compiler_params=_cparams(("parallel", "parallel", "arbitrary" if norm is not None else "parallel")),
        name="gemm",
    )(*args)


def _lora_kernel(*refs, has_v):
    (xw_ref, xa_ref, xg_ref, w1_ref, w2_ref, a1_ref, a2_ref, g1_ref, g2_ref, w0_ref, a0_ref) = refs[:11]
    pos = 11
    if has_v:
        xv_ref, v1_ref, v2_ref, v0_ref, v_ref, vf_ref = refs[pos:pos + 6]
        pos += 6
    lw_ref, a_ref, g_ref = refs[pos:pos + 3]
    pos += 3

    zw = w0_ref[...] + _dot(jnp.tanh(_dot(xw_ref[0], w1_ref[...])), w2_ref[...])
    lw_ref[...] = DECAY_SCALE * _sigmoid(zw)
    a_ref[...] = _sigmoid(a0_ref[...] + _dot(_dot(xa_ref[0], a1_ref[...]), a2_ref[...]))
    g_ref[...] = _dot(_sigmoid(_dot(xg_ref[0], g1_ref[...])), g2_ref[...]).astype(g_ref.dtype)
    if has_v:
        vo_ref = refs[pos]
        mix = _sigmoid(v0_ref[...] + _dot(_dot(xv_ref[0], v1_ref[...]), v2_ref[...]))
        v = v_ref[0]
        vo_ref[...] = v + (vf_ref[0] - v) * mix


def _rwkv_lora(xs, w1, w2, a1, a2, g1, g2, w0, a0, *, vmix=None, tm):
    _, M, C = xs.shape
    row = lambda n: pl.BlockSpec((1, tm, C), lambda i, n=n: (n, i, 0))
    full = lambda arr: pl.BlockSpec(arr.shape, lambda i: (0,) * arr.ndim)
    tile = pl.BlockSpec((tm, C), lambda i: (i, 0))
    vec = lambda z: z.reshape(1, C)
    args = [xs, xs, xs, w1, w2, a1, a2, g1, g2, vec(w0), vec(a0)]
    in_specs = [row(3), row(4), row(5)] + [full(z) for z in args[3:]]
    out_shape = [jax.ShapeDtypeStruct((M, C), F32), jax.ShapeDtypeStruct((M, C), F32),
                 jax.ShapeDtypeStruct((M, C), BF16)]
    out_specs = [tile, tile, tile]
    if vmix is not None:
        v1, v2, v0, rkv, rkv_first = vmix
        extra = [xs, v1, v2, vec(v0), rkv, rkv_first]
        args += extra
        v_tile = pl.BlockSpec((1, tm, C), lambda i: (2, i, 0))
        in_specs += [row(2), full(v1), full(v2), full(extra[3]), v_tile, v_tile]
        out_shape.append(jax.ShapeDtypeStruct((M, C), F32))
        out_specs.append(tile)
    return pl.pallas_call(
        functools.partial(_lora_kernel, has_v=vmix is not None),
        out_shape=out_shape,
        grid=(M // tm,),
        in_specs=in_specs,
        out_specs=out_specs,
        compiler_params=_cparams(("parallel",)),
        name="rwkv_lora",
    )(*args)


def _wkv_masks():
    L, Q = WKV_CHUNK, QUAD
    idx = jnp.arange(Q)
    t = jnp.arange(L)[:, None]
    s = (idx % L)[None, :]
    tri = (t >= jnp.arange(L)[None, :]).astype(BF16)
    strict = (t > s).astype(F32)
    incl = (t >= s).astype(F32)
    block = ((idx[:, None] // L) == (idx[None, :] // L)).astype(BF16)
    return tri, strict, incl, block


def _row_block_sum(z):
    L = WKV_CHUNK
    out = z[0:L]
    for h in range(1, HEADS_PER_QUAD):
        out = out + z[h * L:(h + 1) * L]
    return out


def _block_diag(z, block):
    return jnp.concatenate([z.astype(BF16)] * HEADS_PER_QUAD, axis=0) * block


def _head_sums(zs, block):
    n = zs[0].shape[0]
    parts = []
    for z in zs:
        parts += list(_split2(z))
    out = jnp.dot(jnp.concatenate(parts, axis=0), block, preferred_element_type=F32)
    return [out[2 * i * n:(2 * i + 1) * n] + out[(2 * i + 1) * n:(2 * i + 2) * n] for i in range(len(zs))]


def _wkv_par_kernel(r_ref, k_ref, v_ref, lw_ref, a_ref, kk_w_ref, ka_w_ref, rk_w_ref,
                    tri_ref, strict_ref, incl_ref, block_ref,
                    rp_ref, p2_ref, g_out_ref, h_out_ref, wl_ref, bonus_ref, *, chunks):
    L = WKV_CHUNK
    tri = tri_ref[...]
    block = block_ref[...]
    block_f32 = block.astype(F32)
    strict = strict_ref[...] > 0.5
    incl = incl_ref[...] > 0.5
    eye = incl_ref[...] - strict_ref[...]

    Q = QUAD
    each = range(chunks)
    rows = [slice(c * L, (c + 1) * L) for c in each]
    r = [r_ref[0, rw, :] for rw in rows]
    k = [k_ref[0, rw, :] for rw in rows]
    v = [v_ref[0, rw, :] for rw in rows]
    lw = [lw_ref[rw, :] for rw in rows]
    a_sig = [a_ref[rw, :] for rw in rows]

    parts = []
    for c in each:
        parts += list(_split3(lw[c]))
    sums = jnp.dot(tri, jnp.concatenate(parts, axis=1), preferred_element_type=F32)
    cum = [sums[:, 3 * c * Q:(3 * c + 1) * Q] + sums[:, (3 * c + 1) * Q:(3 * c + 2) * Q]
           + sums[:, (3 * c + 2) * Q:(3 * c + 3) * Q] for c in each]

    k_raw = [k[c] * kk_w_ref[...] for c in each]
    k_mod = [k[c] * (1.0 + (a_sig[c] - 1.0) * ka_w_ref[...]) for c in each]
    head = _head_sums([k_raw[c] * k_raw[c] for c in each]
                      + [r[c] * k_mod[c] * rk_w_ref[...] for c in each], block)
    a_t, r_t, b_end, k_end, n_ab, n_ak, n_rb, n_rk = [], [], [], [], [], [], [], []
    for c in each:
        cum_end = cum[c][L - 1:L, :]
        w_inv = jnp.exp(-cum[c])
        w_to_end = jnp.exp(cum_end - cum[c])
        kk = k_raw[c] / jnp.maximum(jnp.sqrt(head[c]), 1e-12)
        b = kk * a_sig[c]
        a_t.append(-kk * jnp.exp(cum[c] - lw[c]))
        r_t.append(r[c] * jnp.exp(cum[c]))
        b_end.append(b * w_to_end)
        k_end.append(k_mod[c] * w_to_end)
        wl_ref[c] = jnp.exp(cum_end)
        bonus_ref[rows[c], :] = head[chunks + c] * v[c]
        rhs = jnp.concatenate([_block_diag(b * w_inv, block), _block_diag(k_mod[c] * w_inv, block)], axis=0)
        by = _dot_nt(jnp.concatenate([a_t[c], r_t[c]], axis=0), rhs)
        n_ab.append(jnp.where(strict, by[:L, :Q], 0.0))
        n_ak.append(jnp.where(strict, by[:L, Q:], 0.0))
        n_rb.append(jnp.where(incl, by[L:, :Q], 0.0))
        n_rk.append(jnp.where(incl, by[L:, Q:], 0.0))

    v_block = [_block_diag(v[c], block) for c in each]
    p1 = [_dot(n_ak[c], v_block[c]) for c in each]
    p2_v = [_dot(n_rk[c], v_block[c]) for c in each]

    power = list(n_ab)
    t_inv = [eye + n_ab[c] for c in each]
    span = 2
    while span < L:
        for c in each:
            power_block = _block_diag(power[c], block)
            if span == 2:
                power[c] = _dot(power[c], power_block)
            else:
                both = _dot(jnp.concatenate([power[c], t_inv[c]], axis=0), power_block)
                power[c] = both[:L]
                t_inv[c] = t_inv[c] + both[L:]
        span *= 2
    for c in each:
        t_inv[c] = t_inv[c] + _dot(t_inv[c], _block_diag(power[c], block))

    solved = [_dot(t_inv[c], jnp.concatenate([_block_diag(a_t[c], block), _block_diag(p1[c], block)], axis=1))
              for c in each]
    a_p = [z[:, :Q] for z in solved]
    p1_p = [z[:, Q:] for z in solved]
    read = [_dot(n_rb[c], jnp.concatenate([_block_diag(a_p[c], block), _block_diag(p1_p[c], block)], axis=1))
            for c in each]
    for c in each:
        rp_ref[rows[c], :] = (r_t[c] + read[c][:, :Q]).astype(rp_ref.dtype)
        p2_ref[rows[c], :] = read[c][:, Q:] + p2_v[c]
    for c in each:
        g_map = _row_block_sum(_dot_tn(a_p[c], b_end[c]) * block_f32)
        h_map = _row_block_sum(_dot_tn(jnp.concatenate([p1_p[c], v[c]], axis=0),
                                       jnp.concatenate([b_end[c], k_end[c]], axis=0)) * block_f32)
        g_out_ref[rows[c], :] = g_map.astype(g_out_ref.dtype)
        h_out_ref[rows[c], :] = h_map


def _wkv_par(rkv, v_src, lw, a_sig, k_k, k_a, r_k, masks, *, chunks):
    _, M, C = rkv.shape
    v3, v_idx = v_src
    L, Q = WKV_CHUNK, QUAD
    rows = chunks * L
    tile = pl.BlockSpec((rows, Q), lambda i, q: (i, q))
    rk_tile = lambda n: pl.BlockSpec((1, rows, Q), lambda i, q, n=n: (n, i, q))
    vec = pl.BlockSpec((1, Q), lambda i, q: (0, q))
    const = lambda arr: pl.BlockSpec(arr.shape, lambda i, q: (0,) * arr.ndim)
    out_shape = [jax.ShapeDtypeStruct((M, C), BF16), jax.ShapeDtypeStruct((M, C), F32),
                 jax.ShapeDtypeStruct((M, C), BF16), jax.ShapeDtypeStruct((M, C), F32),
                 jax.ShapeDtypeStruct((M // L, 1, C), F32), jax.ShapeDtypeStruct((M, C), F32)]
    out_specs = [tile, tile, tile, tile, pl.BlockSpec((chunks, 1, Q), lambda i, q: (i, 0, q)), tile]
    return pl.pallas_call(
        functools.partial(_wkv_par_kernel, chunks=chunks),
        out_shape=out_shape,
        grid=(M // rows, C // Q),
        in_specs=[rk_tile(0), rk_tile(1), rk_tile(v_idx), tile, tile, vec, vec, vec] + [const(m) for m in masks],
        out_specs=out_specs,
        compiler_params=_cparams(("parallel", "parallel")),
        name="wkv_par",
    )(rkv, rkv, v3, lw, a_sig, k_k.reshape(1, C), k_a.reshape(1, C), r_k.reshape(1, C), *masks)


def _wkv_seq_kernel(rp_ref, p2_ref, gm_ref, hm_ref, wl_ref, bonus_ref, gate_ref, lng_ref, lnb_ref,
                    block_ref, o_ref, s_ref, *, chunks, quads):
    L, Q = WKV_CHUNK, QUAD

    @pl.when(pl.program_id(1) == 0)
    def _():
        s_ref[...] = jnp.zeros_like(s_ref)

    block = block_ref[...]
    inv_n = 1.0 / HEAD_SIZE

    def chunk_body(c, carry):
        rows = pl.ds(pl.multiple_of(c * L, L), L)
        each = range(quads)
        lanes = [slice(q * Q, (q + 1) * Q) for q in each]
        s = [s_ref[q] for q in each]
        y = [_dot_nt(rp_ref[rows, lanes[q]], _block_diag(s[q], block)) + p2_ref[rows, lanes[q]] for q in each]
        for q in each:
            s_ref[q] = (s[q] * wl_ref[c, :, lanes[q]] + _dot(s[q], _block_diag(gm_ref[rows, lanes[q]], block))
                        + hm_ref[rows, lanes[q]])
        mean = _head_sums(y, block)
        d = [y[q] - mean[q] * inv_n for q in each]
        var = _head_sums([z * z for z in d], block)
        for q in each:
            yn = d[q] * lax.rsqrt(var[q] * inv_n + GN_EPS) * lng_ref[:, lanes[q]] + lnb_ref[:, lanes[q]]
            out = (yn + bonus_ref[rows, lanes[q]]) * gate_ref[rows, lanes[q]].astype(F32)
            o_ref[rows, lanes[q]] = out.astype(o_ref.dtype)
        return carry

    lax.fori_loop(0, chunks, chunk_body, 0)


def _wkv_seq(rp, p2, gm, hm, wl, bonus, gate, lnx_g, lnx_b, block, *, chunks, seq_len):
    M, C = rp.shape
    L, Q = WKV_CHUNK, QUAD
    rows = chunks * L
    steps_per_seq = seq_len // rows
    quads = C // Q
    tile = pl.BlockSpec((rows, C), lambda b, i: (b * steps_per_seq + i, 0))
    vec = pl.BlockSpec((1, C), lambda b, i: (0, 0))
    return pl.pallas_call(
        functools.partial(_wkv_seq_kernel, chunks=chunks, quads=quads),
        out_shape=jax.ShapeDtypeStruct((M, C), BF16),
        grid=(M // seq_len, steps_per_seq),
        in_specs=[tile, tile, tile, tile,
                  pl.BlockSpec((chunks, 1, C), lambda b, i: (b * steps_per_seq + i, 0, 0)),
                  tile, tile, vec, vec,
                  pl.BlockSpec(block.shape, lambda b, i: (0, 0))],
        out_specs=tile,
        scratch_shapes=[pltpu.VMEM((quads, L, Q), F32)],
        compiler_params=_cparams(("parallel", "arbitrary")),
        name="wkv_seq",
    )(rp, p2, gm, hm, wl, bonus, gate, lnx_g.reshape(1, C), lnx_b.reshape(1, C), block)


def _conf_mid_kernel(a_ref, b_ref, ah_ref, bh_ref, wdw_ref, bdw_ref, lng_ref, lnb_ref, o_ref,
                     u_scr, c_scr, *, tiles_per_seq):
    tm, C = o_ref.shape
    at_seq_start = (pl.program_id(0) % tiles_per_seq) == 0
    u_halo = ah_ref[0].astype(F32) * _sigmoid(bh_ref[0].astype(F32))
    u_scr[0:CONV_HALO, :] = jnp.where(at_seq_start, 0.0, u_halo)
    u_scr[CONV_HALO:CONV_HALO + tm, :] = a_ref[0].astype(F32) * _sigmoid(b_ref[0].astype(F32))
    first_tap = CONV_HALO - (CONV_WIDTH - 1)

    def slab(s, carry):
        lanes = pl.ds(pl.multiple_of(s * LANES, LANES), LANES)
        window = u_scr[:, lanes]
        acc = jnp.zeros((tm, LANES), F32) + bdw_ref[:, lanes]
        for phase in range(SUBLANES_F32):
            shifted = window if phase == 0 else pltpu.roll(window, CONV_HALO + tm - phase, axis=0)
            for tap in range(CONV_WIDTH):
                start = first_tap + tap - phase
                if start % SUBLANES_F32 == 0:
                    acc = acc + shifted[start:start + tm] * wdw_ref[tap:tap + 1, lanes]
        c_scr[:, lanes] = acc
        return carry

    lax.fori_loop(0, C // LANES, slab, 0)
    y = c_scr[...]
    mu = jnp.mean(y, axis=-1, keepdims=True)
    d = y - mu
    var = jnp.mean(d * d, axis=-1, keepdims=True)
    yn = d * lax.rsqrt(var + LN_EPS) * lng_ref[...] + lnb_ref[...]
    o_ref[...] = _silu(yn).astype(o_ref.dtype)


def _conf_mid(p, w_dw, b_dw, ln_g, ln_b, *, tm, seq_len):
    _, M, C2 = p.shape
    C = C2 // 2
    tiles_per_seq = seq_len // tm
    halo_per_tile = tm // CONV_HALO
    halo_idx = lambda i: jnp.maximum(i * halo_per_tile - 1, 0)
    vec = pl.BlockSpec((1, C), lambda i: (0, 0))
    return pl.pallas_call(
        functools.partial(_conf_mid_kernel, tiles_per_seq=tiles_per_seq),
        out_shape=jax.ShapeDtypeStruct((M, C), BF16),
        grid=(M // tm,),
        in_specs=[pl.BlockSpec((1, tm, C), lambda i: (0, i, 0)),
                  pl.BlockSpec((1, tm, C), lambda i: (0, i, 1)),
                  pl.BlockSpec((1, CONV_HALO, C), lambda i: (0, halo_idx(i), 0)),
                  pl.BlockSpec((1, CONV_HALO, C), lambda i: (0, halo_idx(i), 1)),
                  pl.BlockSpec(w_dw.shape, lambda i: (0, 0)), vec, vec, vec],
        out_specs=pl.BlockSpec((tm, C), lambda i: (i, 0)),
        scratch_shapes=[pltpu.VMEM((CONV_HALO + tm, C), F32), pltpu.VMEM((tm, C), F32)],
        compiler_params=_cparams(("parallel",)),
        name="conformer_mid",
    )(p, p, p, p, w_dw, b_dw.reshape(1, C), ln_g.reshape(1, C), ln_b.reshape(1, C))


def _ffn_up_kernel(x_ref, xh_ref, g_ref, mod_ref, wg_ref, wv_ref, wdw_ref, bdw_ref, o_ref,
                   h_scr, gate_scr, *, rows, tiles_per_seq):
    tm, tn = o_ref.shape

    @pl.when(pl.program_id(1) == 0)
    def _():
        g = g_ref[...]
        at_seq_start = (pl.program_id(0) % tiles_per_seq) == 0
        h_halo = _rms_mod(xh_ref[...], g, mod_ref, rows)
        h_scr[0:FFN_HALO, :] = jnp.where(at_seq_start, 0.0, h_halo).astype(h_scr.dtype)
        h_scr[FFN_HALO:FFN_HALO + tm, :] = _rms_mod(x_ref[...], g, mod_ref, rows).astype(h_scr.dtype)

    gate_scr[...] = jnp.dot(h_scr[...], wg_ref[0], preferred_element_type=F32)
    val = jnp.dot(h_scr[FFN_HALO:FFN_HALO + tm, :], wv_ref[0], preferred_element_type=F32)
    first_tap = FFN_HALO - (FFN_CONV_WIDTH - 1)
    acc = jnp.zeros((tm, tn), F32) + bdw_ref[...]
    for tap in range(FFN_CONV_WIDTH):
        acc = acc + gate_scr[pl.ds(first_tap + tap, tm), :] * wdw_ref[tap:tap + 1, :]
    o_ref[...] = (_silu(acc) * val).astype(o_ref.dtype)


def _ffn_up(x, g, mod_l, w_gv, w_dw, b_dw, *, rows, tm, tn, seq_len):
    M, C = x.shape
    F = w_gv.shape[-1]
    tiles_per_seq = seq_len // tm
    halo_per_tile = tm // FFN_HALO
    return pl.pallas_call(
        functools.partial(_ffn_up_kernel, rows=rows, tiles_per_seq=tiles_per_seq),
        out_shape=jax.ShapeDtypeStruct((M, F), BF16),
        grid=(M // tm, F // tn),
        in_specs=[pl.BlockSpec((tm, C), lambda i, j: (i, 0)),
                  pl.BlockSpec((FFN_HALO, C), lambda i, j: (jnp.maximum(i * halo_per_tile - 1, 0), 0)),
                  pl.BlockSpec((1, C), lambda i, j: (0, 0)),
                  pl.BlockSpec((1,) + mod_l.shape[1:], lambda i, j: (i // tiles_per_seq, 0, 0)),
                  pl.BlockSpec((1, C, tn), lambda i, j: (0, 0, j)),
                  pl.BlockSpec((1, C, tn), lambda i, j: (1, 0, j)),
                  pl.BlockSpec((FFN_CONV_WIDTH, tn), lambda i, j: (0, j)),
                  pl.BlockSpec((1, tn), lambda i, j: (0, j))],
        out_specs=pl.BlockSpec((tm, tn), lambda i, j: (i, j)),
        scratch_shapes=[pltpu.VMEM((FFN_HALO + tm, C), BF16), pltpu.VMEM((FFN_HALO + tm, tn), F32)],
        compiler_params=_cparams(("parallel", "arbitrary")),
        name="ffn_up",
    )(x, x, g.reshape(1, C), mod_l, w_gv, w_gv, w_dw, b_dw.reshape(1, F))


def _pad_to(z, axis, size):
    pad = [(0, 0)] * z.ndim
    pad[axis] = (0, size - z.shape[axis])
    return jnp.pad(z, pad)


def _round_up(n, m):
    return (n + m - 1) // m * m


def _tiles(seq_len, C):
    big = min(1024, seq_len)
    return dict(
        norm=min(512, seq_len), rwkv_in=min(256, seq_len), lora=min(256, seq_len),
        gemm_m=min(512, seq_len), gemm_n=min(2048, C),
        down_m=big, down_n=512,
        conf=min(512, seq_len), ffn_m=big, ffn_n=512,
        wkv_par_chunks=min(8, seq_len // WKV_CHUNK), wkv_seq_chunks=min(4, seq_len // WKV_CHUNK),
        mod_n=1024,
    )


def kernel(x, c, ada_w, ada_b, norm_mix_g, norm_ffn_g, rwkv_mu, rwkv_w_rkv, rwkv_w0, rwkv_w1, rwkv_w2, rwkv_a0, rwkv_a1, rwkv_a2, rwkv_v0, rwkv_v1, rwkv_v2, rwkv_g1, rwkv_g2, rwkv_k_k, rwkv_k_a, rwkv_r_k, rwkv_lnx_g, rwkv_lnx_b, rwkv_w_o, conv_w_pw1, conv_b_pw1, conv_w_dw, conv_b_dw, conv_ln_g, conv_ln_b, conv_w_pw2, conv_b_pw2, ffn_w_up, ffn_w_dw, ffn_b_dw, ffn_w_down, final_norm_g):
    B, T, C = x.shape
    M = B * T
    depth = ada_w.shape[0]
    F = ffn_w_dw.shape[-1]
    t = _tiles(T, C)
    Fp = _round_up(F, t["ffn_n"])
    lora_pad = lambda w, axis: _pad_to(w, axis, _round_up(w.shape[axis], LANES)).astype(BF16)

    xf = x.reshape(M, C)
    c_pad = _pad_to(c, 0, SUBLANES_F32)
    mod = _adaln_mod(c_pad, ada_w, ada_b, tn=min(t["mod_n"], 6 * C))
    mod = mod[:, :B].reshape(depth, B, 6, C)
    masks = _wkv_masks()

    rkv_first = None
    for i in range(depth):
        mod_l = mod[i]
        j = i // 2
        if i % 2 == 0:
            xs = _rwkv_in(xf, norm_mix_g[i], mod_l, rwkv_mu[j], rows=(0, 1), tm=t["rwkv_in"], seq_len=T)
            rkv = _gemm(xs, rwkv_w_rkv[j].astype(BF16), out_dtype=F32, tm=t["gemm_m"], tn=t["gemm_n"])
            lora_w = (lora_pad(rwkv_w1[j], 1), lora_pad(rwkv_w2[j], 0), lora_pad(rwkv_a1[j], 1),
                      lora_pad(rwkv_a2[j], 0), rwkv_g1[j].astype(BF16), rwkv_g2[j].astype(BF16),
                      rwkv_w0[j], rwkv_a0[j])
            if rkv_first is None:
                lw, a_sig, gate = _rwkv_lora(xs, *lora_w, tm=t["lora"])
                v_src = (rkv, 2)
                rkv_first = rkv
            else:
                vmix = (lora_pad(rwkv_v1[j - 1], 1), lora_pad(rwkv_v2[j - 1], 0), rwkv_v0[j - 1], rkv, rkv_first)
                lw, a_sig, gate, v = _rwkv_lora(xs, *lora_w, vmix=vmix, tm=t["lora"])
                v_src = (v[None], 0)
            rp, p2, gm, hm, wl, bonus = _wkv_par(rkv, v_src, lw, a_sig, rwkv_k_k[j], rwkv_k_a[j],
                                                 rwkv_r_k[j].reshape(C), masks, chunks=t["wkv_par_chunks"])
            y = _wkv_seq(rp, p2, gm, hm, wl, bonus, gate, rwkv_lnx_g[j], rwkv_lnx_b[j], masks[3],
                         chunks=t["wkv_seq_chunks"], seq_len=T)
            xf = _gemm(y[None], rwkv_w_o[j].astype(BF16)[None], res=xf, mod_l=mod_l, gate_row=2,
                       out_dtype=F32, tm=t["gemm_m"], tn=t["gemm_n"], seq_len=T)[0]
        else:
            p = _gemm(xf[None], conv_w_pw1[j].astype(BF16)[None], bias=conv_b_pw1[j][None], mod_l=mod_l,
                      norm=(norm_mix_g[i], (0, 1)), out_dtype=BF16, tm=t["gemm_m"], tn=t["gemm_n"], seq_len=T)
            u = _conf_mid(p, conv_w_dw[j], conv_b_dw[j], conv_ln_g[j], conv_ln_b[j], tm=t["conf"], seq_len=T)
            xf = _gemm(u[None], conv_w_pw2[j].astype(BF16)[None], bias=conv_b_pw2[j][None], res=xf,
                       mod_l=mod_l, gate_row=2, out_dtype=F32, tm=t["gemm_m"], tn=t["gemm_n"], seq_len=T)[0]

        w_up = ffn_w_up[i]
        w_gv = jnp.stack([_pad_to(w_up[:, :F], 1, Fp), _pad_to(w_up[:, F:], 1, Fp)]).astype(BF16)
        z = _ffn_up(xf, norm_ffn_g[i], mod_l, w_gv, _pad_to(ffn_w_dw[i], 1, Fp), _pad_to(ffn_b_dw[i], 0, Fp),
                    rows=(3, 4), tm=t["ffn_m"], tn=t["ffn_n"], seq_len=T)
        xf = _gemm(z[None], _pad_to(ffn_w_down[i], 0, Fp).astype(BF16)[None], res=xf, mod_l=mod_l, gate_row=5,
                   out_dtype=F32, tm=t["down_m"], tn=t["down_n"], seq_len=T)[0]

    out = _norm_mod(xf, final_norm_g, mod[0], rows=None, out_dtype=F32, tm=t["norm"], seq_len=T)
    return out.reshape(B, T, C)
```

```python
import functools
import math

import jax
import jax.numpy as jnp
from jax import lax
from jax.experimental import pallas as pl
from jax.experimental.pallas import tpu as pltpu

F32 = jnp.float32
BF16 = jnp.bfloat16

HEAD_SIZE = 64
RMS_EPS = 1e-6
LN_EPS = 1e-5
GN_EPS = 64e-5
DECAY_SCALE = -math.exp(-0.5)
CONV_WIDTH = 31
FFN_CONV_WIDTH = 3

LANES = 128
SUBLANES_F32 = 8
SUBLANES_BF16 = 16
MXU_DIM = 256
VMEM_LIMIT_BYTES = 56 * 1024 * 1024

WKV_CHUNK = HEAD_SIZE
QUAD = MXU_DIM
HEADS_PER_QUAD = QUAD // HEAD_SIZE
CONV_HALO = 32
FFN_HALO = 16


def _cparams(semantics):
    return pltpu.CompilerParams(dimension_semantics=semantics,
                                vmem_limit_bytes=VMEM_LIMIT_BYTES)


def _sigmoid(z):
    return 0.5 * jnp.tanh(0.5 * z) + 0.5


def _silu(z):
    h = 0.5 * z
    return h * jnp.tanh(h) + h


def _dot(a, b):
    return jnp.dot(a.astype(BF16), b.astype(BF16), preferred_element_type=F32)


def _dot_nt(a, b):
    return lax.dot_general(a.astype(BF16), b.astype(BF16), (((1,), (1,)), ((), ())),
                           preferred_element_type=F32)


def _dot_tn(a, b):
    return lax.dot_general(a.astype(BF16), b.astype(BF16), (((0,), (0,)), ((), ())),
                           preferred_element_type=F32)


def _split2(z):
    hi = z.astype(BF16)
    lo = (z - hi.astype(F32)).astype(BF16)
    return hi, lo


def _split3(z):
    hi = z.astype(BF16)
    r1 = z - hi.astype(F32)
    mid = r1.astype(BF16)
    lo = (r1 - mid.astype(F32)).astype(BF16)
    return hi, mid, lo


def _mod_kernel(c_ref, w_ref, b_ref, o_ref):
    c = c_ref[...]
    c_act = _silu(c).astype(BF16)
    o_ref[0] = jnp.dot(c_act, w_ref[0].astype(BF16), preferred_element_type=F32) + b_ref[0]


def _adaln_mod(c_pad, ada_w, ada_b, *, tn):
    depth, C, n6 = ada_w.shape
    rows = c_pad.shape[0]
    return pl.pallas_call(
        _mod_kernel,
        out_shape=jax.ShapeDtypeStruct((depth, rows, n6), F32),
        grid=(depth, n6 // tn),
        in_specs=[pl.BlockSpec((rows, C), lambda l, j: (0, 0)),
                  pl.BlockSpec((1, C, tn), lambda l, j: (l, 0, j)),
                  pl.BlockSpec((1, 1, tn), lambda l, j: (l, 0, j))],
        out_specs=pl.BlockSpec((1, rows, tn), lambda l, j: (l, 0, j)),
        compiler_params=_cparams(("parallel", "parallel")),
        name="adaln_mod",
    )(c_pad, ada_w, ada_b.reshape(depth, 1, n6))


def _rms_mod(x, g, mod_ref, rows):
    y = x * lax.rsqrt(jnp.mean(x * x, axis=-1, keepdims=True) + RMS_EPS) * g
    if rows is not None:
        shift_row, scale_row = rows
        y = y * (1.0 + mod_ref[0, scale_row:scale_row + 1, :]) + mod_ref[0, shift_row:shift_row + 1, :]
    return y


def _norm_kernel(x_ref, g_ref, mod_ref, o_ref, *, rows):
    o_ref[...] = _rms_mod(x_ref[...], g_ref[...], mod_ref, rows).astype(o_ref.dtype)


def _norm_mod(x, g, mod_l, *, rows, out_dtype, tm, seq_len):
    M, C = x.shape
    tiles_per_seq = seq_len // tm
    return pl.pallas_call(
        functools.partial(_norm_kernel, rows=rows),
        out_shape=jax.ShapeDtypeStruct((M, C), out_dtype),
        grid=(M // tm,),
        in_specs=[pl.BlockSpec((tm, C), lambda i: (i, 0)),
                  pl.BlockSpec((1, C), lambda i: (0, 0)),
                  pl.BlockSpec((1,) + mod_l.shape[1:], lambda i: (i // tiles_per_seq, 0, 0))],
        out_specs=pl.BlockSpec((tm, C), lambda i: (i, 0)),
        compiler_params=_cparams(("parallel",)),
        name="norm_mod",
    )(x, g.reshape(1, C), mod_l)


def _rwkv_in_kernel(x_ref, xh_ref, g_ref, mod_ref, o_ref, *, rows, tiles_per_seq):
    i = pl.program_id(0)
    g = g_ref[...]
    h = _rms_mod(x_ref[...], g, mod_ref, rows)
    h_halo = _rms_mod(xh_ref[...], g, mod_ref, rows)
    at_seq_start = (i % tiles_per_seq) == 0
    prev_row = jnp.where(at_seq_start, 0.0, h_halo[SUBLANES_F32 - 1:SUBLANES_F32, :])
    rolled = pltpu.roll(h, 1, axis=0)
    row = lax.broadcasted_iota(jnp.int32, h.shape, 0)
    o_ref[0] = h.astype(o_ref.dtype)
    o_ref[1] = (jnp.where(row == 0, prev_row, rolled) - h).astype(o_ref.dtype)


def _rwkv_in(x, g, mod_l, *, rows, tm, seq_len):
    M, C = x.shape
    tiles_per_seq = seq_len // tm
    halo_blocks_per_tile = tm // SUBLANES_F32
    return pl.pallas_call(
        functools.partial(_rwkv_in_kernel, rows=rows, tiles_per_seq=tiles_per_seq),
        out_shape=jax.ShapeDtypeStruct((2, M, C), BF16),
        grid=(M // tm,),
        in_specs=[pl.BlockSpec((tm, C), lambda i: (i, 0)),
                  pl.BlockSpec((SUBLANES_F32, C),
                               lambda i: (jnp.maximum(i * halo_blocks_per_tile - 1, 0), 0)),
                  pl.BlockSpec((1, C), lambda i: (0, 0)),
                  pl.BlockSpec((1,) + mod_l.shape[1:], lambda i: (i // tiles_per_seq, 0, 0))],
        out_specs=pl.BlockSpec((2, tm, C), lambda i: (0, i, 0)),
        compiler_params=_cparams(("parallel",)),
        name="rwkv_in",
    )(x, x, g.reshape(1, C), mod_l)


def _lerp(hx_ref, mu_row):
    return hx_ref[0] + hx_ref[1] * mu_row.astype(BF16)


def _gemm_kernel(*refs, names, gate_row, norm_rows):
    r = dict(zip(names, refs))
    if "h_scr" in r:
        @pl.when(pl.program_id(2) == 0)
        def _():
            if norm_rows is not None:
                lhs0 = _rms_mod(r["x"][0], r["g"][...], r["mod_norm"], norm_rows)
            else:
                lhs0 = _lerp(r["x"], r["mu"][0])
            r["h_scr"][...] = lhs0.astype(BF16)

        lhs = r["h_scr"][...]
    else:
        lhs = r["x"][0]
    acc = jnp.dot(lhs, r["w"][0], preferred_element_type=F32)
    if "bias" in r:
        acc = acc + r["bias"][0]
    if "w_glu" in r:
        acc = acc * _sigmoid(jnp.dot(lhs, r["w_glu"][0], preferred_element_type=F32) + r["bias_glu"][0])
    if gate_row is not None:
        acc = r["res"][...] + r["mod_gate"][0, gate_row:gate_row + 1, :] * acc
    r["out"][0] = acc.astype(r["out"].dtype)


def _gemm(x, w, *, bias=None, res=None, mod_l=None, gate_row=None, norm=None, mu=None, glu=False,
          out_dtype, tm, tn, seq_len=None):
    nx, M, K = x.shape
    nb, _, N = w.shape
    if glu:
        N = N // 2
    n_col = N // tn
    tiles_per_seq = None if seq_len is None else seq_len // tm
    derived_lhs = norm is not None or mu is not None
    if mu is not None:
        x_spec = pl.BlockSpec((2, tm, K), lambda n, i, j: (0, i, 0))
    elif nx > 1:
        x_spec = pl.BlockSpec((1, tm, K), lambda n, i, j: (n, i, 0))
    else:
        x_spec = pl.BlockSpec((1, tm, K), lambda n, i, j: (0, i, 0))
    operands = [("x", x, x_spec), ("w", w, pl.BlockSpec((1, K, tn), lambda n, i, j: (n, 0, j)))]
    if norm is not None:
        operands.append(("g", norm[0].reshape(1, K), pl.BlockSpec((1, K), lambda n, i, j: (0, 0))))
        operands.append(("mod_norm", mod_l,
                         pl.BlockSpec((1,) + mod_l.shape[1:], lambda n, i, j: (i // tiles_per_seq, 0, 0))))
    if mu is not None:
        operands.append(("mu", mu, pl.BlockSpec((1, 1, K), lambda n, i, j: (n, 0, 0))))
    if bias is not None:
        bias = bias.reshape(nb, 1, -1).astype(F32)
        operands.append(("bias", bias, pl.BlockSpec((1, 1, tn), lambda n, i, j: (n, 0, j))))
    if glu:
        operands.append(("w_glu", w, pl.BlockSpec((1, K, tn), lambda n, i, j: (n, 0, j + n_col))))
        operands.append(("bias_glu", bias, pl.BlockSpec((1, 1, tn), lambda n, i, j: (n, 0, j + n_col))))
    if gate_row is not None:
        operands.append(("res", res, pl.BlockSpec((tm, tn), lambda n, i, j: (i, j))))
        operands.append(("mod_gate", mod_l,
                         pl.BlockSpec((1, mod_l.shape[1], tn), lambda n, i, j: (i // tiles_per_seq, 0, j))))
    names = tuple(name for name, _, _ in operands) + ("out",) + (("h_scr",) if derived_lhs else ())
    return pl.pallas_call(
        functools.partial(_gemm_kernel, names=names, gate_row=gate_row,
                          norm_rows=None if norm is None else norm[1]),
        out_shape=jax.ShapeDtypeStruct((nb, M, N), out_dtype),
        grid=(nb, M // tm, n_col),
        in_specs=[spec for _, _, spec in operands],
        out_specs=pl.BlockSpec((1, tm, tn), lambda n, i, j: (n, i, j)),
        scratch_shapes=[pltpu.VMEM((tm, K), BF16)] if derived_lhs else [],
        compiler_params=_cparams(("parallel", "parallel", "arbitrary" if derived_lhs else "parallel")),
        name="gemm",
    )(*[arr for _, arr, _ in operands])


def _lora_kernel(*refs, has_v):
    (hx_ref, mu_ref, w1_ref, w2_ref, a1_ref, a2_ref, g1_ref, g2_ref, w0_ref, a0_ref) = refs[:10]
    pos = 10
    if has_v:
        v1_ref, v2_ref, v0_ref, v_ref, vf_ref = refs[pos:pos + 5]
        pos += 5
    lw_ref, a_ref, g_ref = refs[pos:pos + 3]
    pos += 3
    xw = _lerp(hx_ref, mu_ref[3:4, :])
    xa = _lerp(hx_ref, mu_ref[4:5, :])
    xg = _lerp(hx_ref, mu_ref[5:6, :])

    zw = w0_ref[...] + _dot(jnp.tanh(_dot(xw, w1_ref[...])), w2_ref[...])
    lw_ref[...] = DECAY_SCALE * _sigmoid(zw)
    a_ref[...] = _sigmoid(a0_ref[...] + _dot(_dot(xa, a1_ref[...]), a2_ref[...]))
    g_ref[...] = _dot(_sigmoid(_dot(xg, g1_ref[...])), g2_ref[...]).astype(g_ref.dtype)
    if has_v:
        vo_ref = refs[pos]
        xv = _lerp(hx_ref, mu_ref[2:3, :])
        mix = _sigmoid(v0_ref[...] + _dot(_dot(xv, v1_ref[...]), v2_ref[...]))
        v = v_ref[0]
        vo_ref[...] = v + (vf_ref[0] - v) * mix


def _rwkv_lora(hx, mu, w1, w2, a1, a2, g1, g2, w0, a0, *, vmix=None, tm):
    _, M, C = hx.shape
    full = lambda arr: pl.BlockSpec(arr.shape, lambda i: (0,) * arr.ndim)
    tile = pl.BlockSpec((tm, C), lambda i: (i, 0))
    vec = lambda z: z.reshape(1, C)
    args = [hx, mu, w1, w2, a1, a2, g1, g2, vec(w0), vec(a0)]
    in_specs = [pl.BlockSpec((2, tm, C), lambda i: (0, i, 0))] + [full(z) for z in args[1:]]
    out_shape = [jax.ShapeDtypeStruct((M, C), F32), jax.ShapeDtypeStruct((M, C), F32),
                 jax.ShapeDtypeStruct((M, C), BF16)]
    out_specs = [tile, tile, tile]
    if vmix is not None:
        v1, v2, v0, rkv, rkv_first = vmix
        extra = [v1, v2, vec(v0), rkv, rkv_first]
        args += extra
        v_tile = pl.BlockSpec((1, tm, C), lambda i: (2, i, 0))
        in_specs += [full(v1), full(v2), full(extra[2]), v_tile, v_tile]
        out_shape.append(jax.ShapeDtypeStruct((M, C), F32))
        out_specs.append(tile)
    return pl.pallas_call(
        functools.partial(_lora_kernel, has_v=vmix is not None),
        out_shape=out_shape,
        grid=(M // tm,),
        in_specs=in_specs,
        out_specs=out_specs,
        compiler_params=_cparams(("parallel",)),
        name="rwkv_lora",
    )(*args)


def _wkv_masks():
    L, Q = WKV_CHUNK, QUAD
    idx = jnp.arange(Q)
    t = jnp.arange(L)[:, None]
    s = (idx % L)[None, :]
    tri = (t >= jnp.arange(L)[None, :]).astype(BF16)
    strict = (t > s).astype(F32)
    incl = (t >= s).astype(F32)
    block = ((idx[:, None] // L) == (idx[None, :] // L)).astype(BF16)
    return tri, strict, incl, block


def _row_block_sum(z):
    L = WKV_CHUNK
    out = z[0:L]
    for h in range(1, HEADS_PER_QUAD):
        out = out + z[h * L:(h + 1) * L]
    return out


def _block_diag(z, block):
    return jnp.concatenate([z.astype(BF16)] * HEADS_PER_QUAD, axis=0) * block


def _head_sums(zs, block):
    n = zs[0].shape[0]
    parts = []
    for z in zs:
        parts += list(_split2(z))
    out = jnp.dot(jnp.concatenate(parts, axis=0), block, preferred_element_type=F32)
    return [out[2 * i * n:(2 * i + 1) * n] + out[(2 * i + 1) * n:(2 * i + 2) * n] for i in range(len(zs))]


def _wkv_par_kernel(r_ref, k_ref, v_ref, lw_ref, a_ref, kk_w_ref, ka_w_ref, rk_w_ref,
                    tri_ref, strict_ref, incl_ref, block_ref,
                    rp_ref, p2_ref, g_out_ref, h_out_ref, wl_ref, bonus_ref, *, chunks):
    L = WKV_CHUNK
    tri = tri_ref[...]
    block = block_ref[...]
    block_f32 = block.astype(F32)
    strict = strict_ref[...] > 0.5
    incl = incl_ref[...] > 0.5
    eye = incl_ref[...] - strict_ref[...]

    Q = QUAD
    each = range(chunks)
    rows = [slice(c * L, (c + 1) * L) for c in each]
    r = [r_ref[0, rw, :] for rw in rows]
    k = [k_ref[0, rw, :] for rw in rows]
    v = [v_ref[0, rw, :] for rw in rows]
    lw = [lw_ref[rw, :] for rw in rows]
    a_sig = [a_ref[rw, :] for rw in rows]

    parts = []
    for c in each:
        parts += list(_split3(lw[c]))
    sums = jnp.dot(tri, jnp.concatenate(parts, axis=1), preferred_element_type=F32)
    cum = [sums[:, 3 * c * Q:(3 * c + 1) * Q] + sums[:, (3 * c + 1) * Q:(3 * c + 2) * Q]
           + sums[:, (3 * c + 2) * Q:(3 * c + 3) * Q] for c in each]

    k_raw = [k[c] * kk_w_ref[...] for c in each]
    k_mod = [k[c] * (1.0 + (a_sig[c] - 1.0) * ka_w_ref[...]) for c in each]
    head = _head_sums([k_raw[c] * k_raw[c] for c in each]
                      + [r[c] * k_mod[c] * rk_w_ref[...] for c in each], block)
    a_t, r_t, b_end, k_end, n_ab, n_ak, n_rb, n_rk = [], [], [], [], [], [], [], []
    for c in each:
        cum_end = cum[c][L - 1:L, :]
        w_inv = jnp.exp(-cum[c])
        w_to_end = jnp.exp(cum_end - cum[c])
        kk = k_raw[c] / jnp.maximum(jnp.sqrt(head[c]), 1e-12)
        b = kk * a_sig[c]
        a_t.append(-kk * jnp.exp(cum[c] - lw[c]))
        r_t.append(r[c] * jnp.exp(cum[c]))
        b_end.append(b * w_to_end)
        k_end.append(k_mod[c] * w_to_end)
        wl_ref[c] = jnp.exp(cum_end)
        bonus_ref[rows[c], :] = head[chunks + c] * v[c]
        rhs = jnp.concatenate([_block_diag(b * w_inv, block), _block_diag(k_mod[c] * w_inv, block)], axis=0)
        by = _dot_nt(jnp.concatenate([a_t[c], r_t[c]], axis=0), rhs)
        n_ab.append(jnp.where(strict, by[:L, :Q], 0.0))
        n_ak.append(jnp.where(strict, by[:L, Q:], 0.0))
        n_rb.append(jnp.where(incl, by[L:, :Q], 0.0))
        n_rk.append(jnp.where(incl, by[L:, Q:], 0.0))

    by_v = [_dot(jnp.concatenate([n_ak[c], n_rk[c]], axis=0), _block_diag(v[c], block)) for c in each]
    p1 = [z[:L] for z in by_v]
    p2_v = [z[L:] for z in by_v]

    power = list(n_ab)
    t_inv = [eye + n_ab[c] for c in each]
    span = 2
    while span < L:
        for c in each:
            power_block = _block_diag(power[c], block)
            if span == 2:
                power[c] = _dot(power[c], power_block)
            else:
                both = _dot(jnp.concatenate([power[c], t_inv[c]], axis=0), power_block)
                power[c] = both[:L]
                t_inv[c] = t_inv[c] + both[L:]
        span *= 2
    for c in each:
        t_inv[c] = t_inv[c] + _dot(t_inv[c], _block_diag(power[c], block))

    read_inv = [_dot(n_rb[c], _block_diag(t_inv[c], block)) for c in each]
    solved = [_dot(jnp.concatenate([t_inv[c], read_inv[c]], axis=0),
                   jnp.concatenate([_block_diag(a_t[c], block), _block_diag(p1[c], block)], axis=1))
              for c in each]
    a_p = [z[:L, :Q] for z in solved]
    p1_p = [z[:L, Q:] for z in solved]
    for c in each:
        rp_ref[rows[c], :] = (r_t[c] + solved[c][L:, :Q]).astype(rp_ref.dtype)
        p2_ref[rows[c], :] = solved[c][L:, Q:] + p2_v[c]
    for c in each:
        g_map = _row_block_sum(_dot_tn(a_p[c], b_end[c]) * block_f32)
        h_map = _row_block_sum(_dot_tn(jnp.concatenate([p1_p[c], v[c]], axis=0),
                                       jnp.concatenate([b_end[c], k_end[c]], axis=0)) * block_f32)
        g_out_ref[rows[c], :] = g_map.astype(g_out_ref.dtype)
        h_out_ref[rows[c], :] = h_map


def _wkv_par(rkv, v_src, lw, a_sig, k_k, k_a, r_k, masks, *, chunks):
    _, M, C = rkv.shape
    v3, v_idx = v_src
    L, Q = WKV_CHUNK, QUAD
    rows = chunks * L
    tile = pl.BlockSpec((rows, Q), lambda i, q: (i, q))
    rk_tile = lambda n: pl.BlockSpec((1, rows, Q), lambda i, q, n=n: (n, i, q))
    vec = pl.BlockSpec((1, Q), lambda i, q: (0, q))
    const = lambda arr: pl.BlockSpec(arr.shape, lambda i, q: (0,) * arr.ndim)
    out_shape = [jax.ShapeDtypeStruct((M, C), BF16), jax.ShapeDtypeStruct((M, C), F32),
                 jax.ShapeDtypeStruct((M, C), BF16), jax.ShapeDtypeStruct((M, C), F32),
                 jax.ShapeDtypeStruct((M // L, 1, C), F32), jax.ShapeDtypeStruct((M, C), F32)]
    out_specs = [tile, tile, tile, tile, pl.BlockSpec((chunks, 1, Q), lambda i, q: (i, 0, q)), tile]
    return pl.pallas_call(
        functools.partial(_wkv_par_kernel, chunks=chunks),
        out_shape=out_shape,
        grid=(M // rows, C // Q),
        in_specs=[rk_tile(0), rk_tile(1), rk_tile(v_idx), tile, tile, vec, vec, vec] + [const(m) for m in masks],
        out_specs=out_specs,
        compiler_params=_cparams(("parallel", "parallel")),
        name="wkv_par",
    )(rkv, rkv, v3, lw, a_sig, k_k.reshape(1, C), k_a.reshape(1, C), r_k.reshape(1, C), *masks)


def _wkv_seq_kernel(rp_ref, p2_ref, gm_ref, hm_ref, wl_ref, bonus_ref, gate_ref, lng_ref, lnb_ref,
                    block_ref, o_ref, s_ref, *, chunks, quads):
    L, Q = WKV_CHUNK, QUAD

    @pl.when(pl.program_id(1) == 0)
    def _():
        s_ref[...] = jnp.zeros_like(s_ref)

    block = block_ref[...]
    inv_n = 1.0 / HEAD_SIZE

    def chunk_body(c, carry):
        rows = pl.ds(pl.multiple_of(c * L, L), L)
        each = range(quads)
        lanes = [slice(q * Q, (q + 1) * Q) for q in each]
        s = [s_ref[q] for q in each]
        y = [_dot_nt(rp_ref[rows, lanes[q]], _block_diag(s[q], block)) + p2_ref[rows, lanes[q]] for q in each]
        for q in each:
            s_ref[q] = (s[q] * wl_ref[c, :, lanes[q]] + _dot(s[q], _block_diag(gm_ref[rows, lanes[q]], block))
                        + hm_ref[rows, lanes[q]])
        mean = _head_sums(y, block)
        d = [y[q] - mean[q] * inv_n for q in each]
        var = _head_sums([z * z for z in d], block)
        for q in each:
            yn = d[q] * lax.rsqrt(var[q] * inv_n + GN_EPS) * lng_ref[:, lanes[q]] + lnb_ref[:, lanes[q]]
            out = (yn + bonus_ref[rows, lanes[q]]) * gate_ref[rows, lanes[q]].astype(F32)
            o_ref[rows, lanes[q]] = out.astype(o_ref.dtype)
        return carry

    lax.fori_loop(0, chunks, chunk_body, 0)


def _wkv_seq(rp, p2, gm, hm, wl, bonus, gate, lnx_g, lnx_b, block, *, chunks, seq_len):
    M, C = rp.shape
    L, Q = WKV_CHUNK, QUAD
    rows = chunks * L
    steps_per_seq = seq_len // rows
    quads = C // Q
    tile = pl.BlockSpec((rows, C), lambda b, i: (b * steps_per_seq + i, 0))
    vec = pl.BlockSpec((1, C), lambda b, i: (0, 0))
    return pl.pallas_call(
        functools.partial(_wkv_seq_kernel, chunks=chunks, quads=quads),
        out_shape=jax.ShapeDtypeStruct((M, C), BF16),
        grid=(M // seq_len, steps_per_seq),
        in_specs=[tile, tile, tile, tile,
                  pl.BlockSpec((chunks, 1, C), lambda b, i: (b * steps_per_seq + i, 0, 0)),
                  tile, tile, vec, vec,
                  pl.BlockSpec(block.shape, lambda b, i: (0, 0))],
        out_specs=tile,
        scratch_shapes=[pltpu.VMEM((quads, L, Q), F32)],
        compiler_params=_cparams(("parallel", "arbitrary")),
        name="wkv_seq",
    )(rp, p2, gm, hm, wl, bonus, gate, lnx_g.reshape(1, C), lnx_b.reshape(1, C), block)


def _conf_mid_kernel(u_ref, uh_ref, wdw_ref, bdw_ref, lng_ref, lnb_ref, o_ref,
                     u_scr, c_scr, *, tiles_per_seq):
    tm, C = o_ref.shape
    at_seq_start = (pl.program_id(0) % tiles_per_seq) == 0
    u_scr[0:CONV_HALO, :] = jnp.where(at_seq_start, 0.0, uh_ref[0].astype(F32))
    u_scr[CONV_HALO:CONV_HALO + tm, :] = u_ref[0].astype(F32)
    first_tap = CONV_HALO - (CONV_WIDTH - 1)

    def slab(s, carry):
        lanes = pl.ds(pl.multiple_of(s * LANES, LANES), LANES)
        window = u_scr[:, lanes]
        acc = jnp.zeros((tm, LANES), F32) + bdw_ref[:, lanes]
        for phase in range(SUBLANES_F32):
            shifted = window if phase == 0 else pltpu.roll(window, CONV_HALO + tm - phase, axis=0)
            for tap in range(CONV_WIDTH):
                start = first_tap + tap - phase
                if start % SUBLANES_F32 == 0:
                    acc = acc + shifted[start:start + tm] * wdw_ref[tap:tap + 1, lanes]
        c_scr[:, lanes] = acc
        return carry

    lax.fori_loop(0, C // LANES, slab, 0)
    y = c_scr[...]
    mu = jnp.mean(y, axis=-1, keepdims=True)
    d = y - mu
    var = jnp.mean(d * d, axis=-1, keepdims=True)
    yn = d * lax.rsqrt(var + LN_EPS) * lng_ref[...] + lnb_ref[...]
    o_ref[...] = _silu(yn).astype(o_ref.dtype)


def _conf_mid(u, w_dw, b_dw, ln_g, ln_b, *, tm, seq_len):
    _, M, C = u.shape
    tiles_per_seq = seq_len // tm
    halo_per_tile = tm // CONV_HALO
    halo_idx = lambda i: jnp.maximum(i * halo_per_tile - 1, 0)
    vec = pl.BlockSpec((1, C), lambda i: (0, 0))
    return pl.pallas_call(
        functools.partial(_conf_mid_kernel, tiles_per_seq=tiles_per_seq),
        out_shape=jax.ShapeDtypeStruct((M, C), BF16),
        grid=(M // tm,),
        in_specs=[pl.BlockSpec((1, tm, C), lambda i: (0, i, 0)),
                  pl.BlockSpec((1, CONV_HALO, C), lambda i: (0, halo_idx(i), 0)),
                  pl.BlockSpec(w_dw.shape, lambda i: (0, 0)), vec, vec, vec],
        out_specs=pl.BlockSpec((tm, C), lambda i: (i, 0)),
        scratch_shapes=[pltpu.VMEM((CONV_HALO + tm, C), F32), pltpu.VMEM((tm, C), F32)],
        compiler_params=_cparams(("parallel",)),
        name="conformer_mid",
    )(u, u, w_dw, b_dw.reshape(1, C), ln_g.reshape(1, C), ln_b.reshape(1, C))


def _ffn_up_kernel(x_ref, xh_ref, g_ref, mod_ref, wg_ref, wv_ref, wdw_ref, bdw_ref, o_ref,
                   h_scr, *gate_scrs, rows, tiles_per_seq):
    tm, tn = o_ref.shape
    sub = tm // len(gate_scrs)

    @pl.when(pl.program_id(1) == 0)
    def _():
        g = g_ref[...]
        at_seq_start = (pl.program_id(0) % tiles_per_seq) == 0
        h_halo = _rms_mod(xh_ref[...], g, mod_ref, rows)
        h_scr[0:FFN_HALO, :] = jnp.where(at_seq_start, 0.0, h_halo).astype(h_scr.dtype)
        h_scr[FFN_HALO:FFN_HALO + tm, :] = _rms_mod(x_ref[...], g, mod_ref, rows).astype(h_scr.dtype)

    vals = []
    for n, gate_scr in enumerate(gate_scrs):
        gate_scr[...] = jnp.dot(h_scr[n * sub:n * sub + FFN_HALO + sub, :], wg_ref[0],
                                preferred_element_type=F32)
        vals.append(jnp.dot(h_scr[FFN_HALO + n * sub:FFN_HALO + (n + 1) * sub, :], wv_ref[0],
                            preferred_element_type=F32))
    first_tap = FFN_HALO - (FFN_CONV_WIDTH - 1)
    for n, gate_scr in enumerate(gate_scrs):
        acc = jnp.zeros((sub, tn), F32) + bdw_ref[...]
        for tap in range(FFN_CONV_WIDTH):
            acc = acc + gate_scr[pl.ds(first_tap + tap, sub), :] * wdw_ref[tap:tap + 1, :]
        o_ref[n * sub:(n + 1) * sub, :] = (_silu(acc) * vals[n]).astype(o_ref.dtype)


def _ffn_up(x, g, mod_l, w_gv, w_dw, b_dw, *, rows, tm, tn, row_splits, seq_len):
    M, C = x.shape
    F = w_gv.shape[-1]
    tiles_per_seq = seq_len // tm
    halo_per_tile = tm // FFN_HALO
    return pl.pallas_call(
        functools.partial(_ffn_up_kernel, rows=rows, tiles_per_seq=tiles_per_seq),
        out_shape=jax.ShapeDtypeStruct((M, F), BF16),
        grid=(M // tm, F // tn),
        in_specs=[pl.BlockSpec((tm, C), lambda i, j: (i, 0)),
                  pl.BlockSpec((FFN_HALO, C), lambda i, j: (jnp.maximum(i * halo_per_tile - 1, 0), 0)),
                  pl.BlockSpec((1, C), lambda i, j: (0, 0)),
                  pl.BlockSpec((1,) + mod_l.shape[1:], lambda i, j: (i // tiles_per_seq, 0, 0)),
                  pl.BlockSpec((1, C, tn), lambda i, j: (0, 0, j)),
                  pl.BlockSpec((1, C, tn), lambda i, j: (1, 0, j)),
                  pl.BlockSpec((FFN_CONV_WIDTH, tn), lambda i, j: (0, j)),
                  pl.BlockSpec((1, tn), lambda i, j: (0, j))],
        out_specs=pl.BlockSpec((tm, tn), lambda i, j: (i, j)),
        scratch_shapes=([pltpu.VMEM((FFN_HALO + tm, C), BF16)]
                        + [pltpu.VMEM((FFN_HALO + tm // row_splits, tn), F32)] * row_splits),
        compiler_params=_cparams(("parallel", "arbitrary")),
        name="ffn_up",
    )(x, x, g.reshape(1, C), mod_l, w_gv, w_gv, w_dw, b_dw.reshape(1, F))


def _pad_to(z, axis, size):
    pad = [(0, 0)] * z.ndim
    pad[axis] = (0, size - z.shape[axis])
    return jnp.pad(z, pad)


def _round_up(n, m):
    return (n + m - 1) // m * m


def _tiles(seq_len, C):
    big = min(1024, seq_len)
    return dict(
        norm=min(512, seq_len), rwkv_in=min(256, seq_len), lora=min(256, seq_len),
        gemm_m=min(512, seq_len), gemm_n=min(2048, C),
        down_m=big, down_n=512,
        glu_n=min(1024, C),
        conf=min(512, seq_len), ffn_m=big, ffn_n=512, ffn_row_splits=1,
        wkv_par_chunks=min(16, seq_len // WKV_CHUNK), wkv_seq_chunks=min(4, seq_len // WKV_CHUNK),
        mod_n=1024,
    )


def kernel(x, c, ada_w, ada_b, norm_mix_g, norm_ffn_g, rwkv_mu, rwkv_w_rkv, rwkv_w0, rwkv_w1, rwkv_w2, rwkv_a0, rwkv_a1, rwkv_a2, rwkv_v0, rwkv_v1, rwkv_v2, rwkv_g1, rwkv_g2, rwkv_k_k, rwkv_k_a, rwkv_r_k, rwkv_lnx_g, rwkv_lnx_b, rwkv_w_o, conv_w_pw1, conv_b_pw1, conv_w_dw, conv_b_dw, conv_ln_g, conv_ln_b, conv_w_pw2, conv_b_pw2, ffn_w_up, ffn_w_dw, ffn_b_dw, ffn_w_down, final_norm_g):
    B, T, C = x.shape
    M = B * T
    depth = ada_w.shape[0]
    F = ffn_w_dw.shape[-1]
    t = _tiles(T, C)
    Fp = _round_up(F, t["ffn_n"])
    lora_pad = lambda w, axis: _pad_to(w, axis, _round_up(w.shape[axis], LANES)).astype(BF16)

    xf = x.reshape(M, C)
    c_pad = _pad_to(c, 0, SUBLANES_F32)
    mod = _adaln_mod(c_pad, ada_w, ada_b, tn=min(t["mod_n"], 6 * C))
    mod = mod[:, :B].reshape(depth, B, 6, C)
    masks = _wkv_masks()

    rkv_first = None
    for i in range(depth):
        mod_l = mod[i]
        j = i // 2
        if i % 2 == 0:
            hx = _rwkv_in(xf, norm_mix_g[i], mod_l, rows=(0, 1), tm=t["rwkv_in"], seq_len=T)
            mu = rwkv_mu[j]
            rkv = _gemm(hx, rwkv_w_rkv[j].astype(BF16), mu=mu[:3, None, :], out_dtype=F32,
                        tm=t["gemm_m"], tn=t["gemm_n"])
            lora_w = (lora_pad(rwkv_w1[j], 1), lora_pad(rwkv_w2[j], 0), lora_pad(rwkv_a1[j], 1),
                      lora_pad(rwkv_a2[j], 0), rwkv_g1[j].astype(BF16), rwkv_g2[j].astype(BF16),
                      rwkv_w0[j], rwkv_a0[j])
            if rkv_first is None:
                lw, a_sig, gate = _rwkv_lora(hx, mu, *lora_w, tm=t["lora"])
                v_src = (rkv, 2)
                rkv_first = rkv
            else:
                vmix = (lora_pad(rwkv_v1[j - 1], 1), lora_pad(rwkv_v2[j - 1], 0), rwkv_v0[j - 1], rkv, rkv_first)
                lw, a_sig, gate, v = _rwkv_lora(hx, mu, *lora_w, vmix=vmix, tm=t["lora"])
                v_src = (v[None], 0)
            rp, p2, gm, hm, wl, bonus = _wkv_par(rkv, v_src, lw, a_sig, rwkv_k_k[j], rwkv_k_a[j],
                                                 rwkv_r_k[j].reshape(C), masks, chunks=t["wkv_par_chunks"])
            y = _wkv_seq(rp, p2, gm, hm, wl, bonus, gate, rwkv_lnx_g[j], rwkv_lnx_b[j], masks[3],
                         chunks=t["wkv_seq_chunks"], seq_len=T)
            xf = _gemm(y[None], rwkv_w_o[j].astype(BF16)[None], res=xf, mod_l=mod_l, gate_row=2,
                       out_dtype=F32, tm=t["gemm_m"], tn=t["gemm_n"], seq_len=T)[0]
        else:
            u = _gemm(xf[None], conv_w_pw1[j].astype(BF16)[None], bias=conv_b_pw1[j][None], mod_l=mod_l,
                      norm=(norm_mix_g[i], (0, 1)), glu=True, out_dtype=BF16, tm=t["gemm_m"], tn=t["glu_n"],
                      seq_len=T)
            u = _conf_mid(u, conv_w_dw[j], conv_b_dw[j], conv_ln_g[j], conv_ln_b[j], tm=t["conf"], seq_len=T)
            xf = _gemm(u[None], conv_w_pw2[j].astype(BF16)[None], bias=conv_b_pw2[j][None], res=xf,
                       mod_l=mod_l, gate_row=2, out_dtype=F32, tm=t["gemm_m"], tn=t["gemm_n"], seq_len=T)[0]

        w_up = ffn_w_up[i]
        w_gv = jnp.stack([_pad_to(w_up[:, :F], 1, Fp), _pad_to(w_up[:, F:], 1, Fp)]).astype(BF16)
        z = _ffn_up(xf, norm_ffn_g[i], mod_l, w_gv, _pad_to(ffn_w_dw[i], 1, Fp), _pad_to(ffn_b_dw[i], 0, Fp),
                    rows=(3, 4), tm=t["ffn_m"], tn=t["ffn_n"], row_splits=t["ffn_row_splits"], seq_len=T)
        xf = _gemm(z[None], _pad_to(ffn_w_down[i], 0, Fp).astype(BF16)[None], res=xf, mod_l=mod_l, gate_row=5,
                   out_dtype=F32, tm=t["down_m"], tn=t["down_n"], seq_len=T)[0]

    out = _norm_mod(xf, final_norm_g, mod[0], rows=None, out_dtype=F32, tm=t["norm"], seq_len=T)
    return out.reshape(B, T, C)
```

```python
import functools
import math

import jax
import jax.numpy as jnp
from jax import lax
from jax.experimental import pallas as pl
from jax.experimental.pallas import tpu as pltpu

F32 = jnp.float32
BF16 = jnp.bfloat16

HEAD_SIZE = 64
RMS_EPS = 1e-6
LN_EPS = 1e-5
GN_EPS = 64e-5
DECAY_SCALE = -math.exp(-0.5)
CONV_WIDTH = 31
FFN_CONV_WIDTH = 3

LANES = 128
SUBLANES_F32 = 8
SUBLANES_BF16 = 16
MXU_DIM = 256
VMEM_LIMIT_BYTES = 56 * 1024 * 1024

WKV_CHUNK = HEAD_SIZE
QUAD = MXU_DIM
HEADS_PER_QUAD = QUAD // HEAD_SIZE
CONV_HALO = 32
FFN_HALO = 16


def _cparams(semantics):
    return pltpu.CompilerParams(dimension_semantics=semantics,
                                vmem_limit_bytes=VMEM_LIMIT_BYTES)


def _sigmoid(z):
    return 0.5 * jnp.tanh(0.5 * z) + 0.5


def _silu(z):
    h = 0.5 * z
    return h * jnp.tanh(h) + h


def _dot(a, b):
    return jnp.dot(a.astype(BF16), b.astype(BF16), preferred_element_type=F32)


def _dot_nt(a, b):
    return lax.dot_general(a.astype(BF16), b.astype(BF16), (((1,), (1,)), ((), ())),
                           preferred_element_type=F32)


def _dot_tn(a, b):
    return lax.dot_general(a.astype(BF16), b.astype(BF16), (((0,), (0,)), ((), ())),
                           preferred_element_type=F32)


def _split2(z):
    hi = z.astype(BF16)
    lo = (z - hi.astype(F32)).astype(BF16)
    return hi, lo


def _split3(z):
    hi = z.astype(BF16)
    r1 = z - hi.astype(F32)
    mid = r1.astype(BF16)
    lo = (r1 - mid.astype(F32)).astype(BF16)
    return hi, mid, lo


def _mod_kernel(c_ref, w_ref, b_ref, o_ref):
    c = c_ref[...]
    c_act = _silu(c).astype(BF16)
    o_ref[0] = jnp.dot(c_act, w_ref[0].astype(BF16), preferred_element_type=F32) + b_ref[0]


def _adaln_mod(c_pad, ada_w, ada_b, *, tn):
    depth, C, n6 = ada_w.shape
    rows = c_pad.shape[0]
    return pl.pallas_call(
        _mod_kernel,
        out_shape=jax.ShapeDtypeStruct((depth, rows, n6), F32),
        grid=(depth, n6 // tn),
        in_specs=[pl.BlockSpec((rows, C), lambda l, j: (0, 0)),
                  pl.BlockSpec((1, C, tn), lambda l, j: (l, 0, j)),
                  pl.BlockSpec((1, 1, tn), lambda l, j: (l, 0, j))],
        out_specs=pl.BlockSpec((1, rows, tn), lambda l, j: (l, 0, j)),
        compiler_params=_cparams(("parallel", "parallel")),
        name="adaln_mod",
    )(c_pad, ada_w, ada_b.reshape(depth, 1, n6))


def _rms_mod(x, g, mod_ref, rows):
    y = x * lax.rsqrt(jnp.mean(x * x, axis=-1, keepdims=True) + RMS_EPS) * g
    if rows is not None:
        shift_row, scale_row = rows
        y = y * (1.0 + mod_ref[0, scale_row:scale_row + 1, :]) + mod_ref[0, shift_row:shift_row + 1, :]
    return y


def _norm_kernel(x_ref, g_ref, mod_ref, o_ref, *, rows):
    o_ref[...] = _rms_mod(x_ref[...], g_ref[...], mod_ref, rows).astype(o_ref.dtype)


def _norm_mod(x, g, mod_l, *, rows, out_dtype, tm, seq_len):
    M, C = x.shape
    tiles_per_seq = seq_len // tm
    return pl.pallas_call(
        functools.partial(_norm_kernel, rows=rows),
        out_shape=jax.ShapeDtypeStruct((M, C), out_dtype),
        grid=(M // tm,),
        in_specs=[pl.BlockSpec((tm, C), lambda i: (i, 0)),
                  pl.BlockSpec((1, C), lambda i: (0, 0)),
                  pl.BlockSpec((1,) + mod_l.shape[1:], lambda i: (i // tiles_per_seq, 0, 0))],
        out_specs=pl.BlockSpec((tm, C), lambda i: (i, 0)),
        compiler_params=_cparams(("parallel",)),
        name="norm_mod",
    )(x, g.reshape(1, C), mod_l)


def _rwkv_in_kernel(x_ref, xh_ref, g_ref, mod_ref, o_ref, *, rows, tiles_per_seq):
    i = pl.program_id(0)
    g = g_ref[...]
    h = _rms_mod(x_ref[...], g, mod_ref, rows)
    h_halo = _rms_mod(xh_ref[...], g, mod_ref, rows)
    at_seq_start = (i % tiles_per_seq) == 0
    prev_row = jnp.where(at_seq_start, 0.0, h_halo[SUBLANES_F32 - 1:SUBLANES_F32, :])
    rolled = pltpu.roll(h, 1, axis=0)
    row = lax.broadcasted_iota(jnp.int32, h.shape, 0)
    o_ref[0] = h.astype(o_ref.dtype)
    o_ref[1] = (jnp.where(row == 0, prev_row, rolled) - h).astype(o_ref.dtype)


def _rwkv_in(x, g, mod_l, *, rows, tm, seq_len):
    M, C = x.shape
    tiles_per_seq = seq_len // tm
    halo_blocks_per_tile = tm // SUBLANES_F32
    return pl.pallas_call(
        functools.partial(_rwkv_in_kernel, rows=rows, tiles_per_seq=tiles_per_seq),
        out_shape=jax.ShapeDtypeStruct((2, M, C), BF16),
        grid=(M // tm,),
        in_specs=[pl.BlockSpec((tm, C), lambda i: (i, 0)),
                  pl.BlockSpec((SUBLANES_F32, C),
                               lambda i: (jnp.maximum(i * halo_blocks_per_tile - 1, 0), 0)),
                  pl.BlockSpec((1, C), lambda i: (0, 0)),
                  pl.BlockSpec((1,) + mod_l.shape[1:], lambda i: (i // tiles_per_seq, 0, 0))],
        out_specs=pl.BlockSpec((2, tm, C), lambda i: (0, i, 0)),
        compiler_params=_cparams(("parallel",)),
        name="rwkv_in",
    )(x, x, g.reshape(1, C), mod_l)


def _lerp(hx_ref, mu_row):
    return hx_ref[0] + hx_ref[1] * mu_row.astype(BF16)


def _gemm_kernel(*refs, names, gate_row, norm_rows):
    r = dict(zip(names, refs))
    if "h_scr" in r:
        @pl.when(pl.program_id(2) == 0)
        def _():
            if norm_rows is not None:
                lhs0 = _rms_mod(r["x"][0], r["g"][...], r["mod_norm"], norm_rows)
            else:
                lhs0 = _lerp(r["x"], r["mu"][0])
            r["h_scr"][...] = lhs0.astype(BF16)

        lhs = r["h_scr"][...]
    else:
        lhs = r["x"][0]
    acc = jnp.dot(lhs, r["w"][0], preferred_element_type=F32)
    if "bias" in r:
        acc = acc + r["bias"][0]
    if "w_glu" in r:
        acc = acc * _sigmoid(jnp.dot(lhs, r["w_glu"][0], preferred_element_type=F32) + r["bias_glu"][0])
    if gate_row is not None:
        acc = r["res"][...] + r["mod_gate"][0, gate_row:gate_row + 1, :] * acc
    r["out"][0] = acc.astype(r["out"].dtype)


def _gemm(x, w, *, bias=None, res=None, mod_l=None, gate_row=None, norm=None, mu=None, glu=False, w_lead=None,
          out_dtype, tm, tn, seq_len=None):
    nx, M, K = x.shape
    nb, _, N = w.shape
    w_spec = pl.BlockSpec((1, K, tn), lambda n, i, j: (n, 0, j))
    if w_lead is not None:
        nb = 1
        w_spec = pl.BlockSpec((1, K, tn), lambda n, i, j: (w_lead, 0, j))
    if glu:
        N = N // 2
    n_col = N // tn
    tiles_per_seq = None if seq_len is None else seq_len // tm
    derived_lhs = norm is not None or mu is not None
    if mu is not None:
        x_spec = pl.BlockSpec((2, tm, K), lambda n, i, j: (0, i, 0))
    elif nx > 1:
        x_spec = pl.BlockSpec((1, tm, K), lambda n, i, j: (n, i, 0))
    else:
        x_spec = pl.BlockSpec((1, tm, K), lambda n, i, j: (0, i, 0))
    operands = [("x", x, x_spec), ("w", w, w_spec)]
    if norm is not None:
        operands.append(("g", norm[0].reshape(1, K), pl.BlockSpec((1, K), lambda n, i, j: (0, 0))))
        operands.append(("mod_norm", mod_l,
                         pl.BlockSpec((1,) + mod_l.shape[1:], lambda n, i, j: (i // tiles_per_seq, 0, 0))))
    if mu is not None:
        operands.append(("mu", mu, pl.BlockSpec((1, 1, K), lambda n, i, j: (n, 0, 0))))
    if bias is not None:
        bias = bias.reshape(nb, 1, -1).astype(F32)
        operands.append(("bias", bias, pl.BlockSpec((1, 1, tn), lambda n, i, j: (n, 0, j))))
    if glu:
        operands.append(("w_glu", w, pl.BlockSpec((1, K, tn), lambda n, i, j: (n, 0, j + n_col))))
        operands.append(("bias_glu", bias, pl.BlockSpec((1, 1, tn), lambda n, i, j: (n, 0, j + n_col))))
    if gate_row is not None:
        operands.append(("res", res, pl.BlockSpec((tm, tn), lambda n, i, j: (i, j))))
        operands.append(("mod_gate", mod_l,
                         pl.BlockSpec((1, mod_l.shape[1], tn), lambda n, i, j: (i // tiles_per_seq, 0, j))))
    names = tuple(name for name, _, _ in operands) + ("out",) + (("h_scr",) if derived_lhs else ())
    return pl.pallas_call(
        functools.partial(_gemm_kernel, names=names, gate_row=gate_row,
                          norm_rows=None if norm is None else norm[1]),
        out_shape=jax.ShapeDtypeStruct((nb, M, N), out_dtype),
        grid=(nb, M // tm, n_col),
        in_specs=[spec for _, _, spec in operands],
        out_specs=pl.BlockSpec((1, tm, tn), lambda n, i, j: (n, i, j)),
        scratch_shapes=[pltpu.VMEM((tm, K), BF16)] if derived_lhs else [],
        compiler_params=_cparams(("parallel", "parallel", "arbitrary" if derived_lhs else "parallel")),
        name="gemm",
    )(*[arr for _, arr, _ in operands])


def _lora_kernel(*refs, has_v):
    (hx_ref, mu_ref, w1_ref, w2_ref, a1_ref, a2_ref, g1_ref, g2_ref, w0_ref, a0_ref) = refs[:10]
    pos = 10
    if has_v:
        v1_ref, v2_ref, v0_ref, v_ref, vf_ref = refs[pos:pos + 5]
        pos += 5
    lw_ref, a_ref, g_ref = refs[pos:pos + 3]
    pos += 3
    xw = _lerp(hx_ref, mu_ref[3:4, :])
    xa = _lerp(hx_ref, mu_ref[4:5, :])
    xg = _lerp(hx_ref, mu_ref[5:6, :])

    zw = w0_ref[...] + _dot(jnp.tanh(_dot(xw, w1_ref[...])), w2_ref[...])
    lw_ref[...] = DECAY_SCALE * _sigmoid(zw)
    a_ref[...] = _sigmoid(a0_ref[...] + _dot(_dot(xa, a1_ref[...]), a2_ref[...]))
    g_ref[...] = _dot(_sigmoid(_dot(xg, g1_ref[...])), g2_ref[...]).astype(g_ref.dtype)
    if has_v:
        vo_ref = refs[pos]
        xv = _lerp(hx_ref, mu_ref[2:3, :])
        mix = _sigmoid(v0_ref[...] + _dot(_dot(xv, v1_ref[...]), v2_ref[...]))
        v = v_ref[0]
        vo_ref[...] = v + (vf_ref[0] - v) * mix


def _rwkv_lora(hx, mu, w1, w2, a1, a2, g1, g2, w0, a0, *, vmix=None, tm):
    _, M, C = hx.shape
    full = lambda arr: pl.BlockSpec(arr.shape, lambda i: (0,) * arr.ndim)
    tile = pl.BlockSpec((tm, C), lambda i: (i, 0))
    vec = lambda z: z.reshape(1, C)
    args = [hx, mu, w1, w2, a1, a2, g1, g2, vec(w0), vec(a0)]
    in_specs = [pl.BlockSpec((2, tm, C), lambda i: (0, i, 0))] + [full(z) for z in args[1:]]
    out_shape = [jax.ShapeDtypeStruct((M, C), F32), jax.ShapeDtypeStruct((M, C), F32),
                 jax.ShapeDtypeStruct((M, C), BF16)]
    out_specs = [tile, tile, tile]
    if vmix is not None:
        v1, v2, v0, rkv, rkv_first = vmix
        extra = [v1, v2, vec(v0), rkv, rkv_first]
        args += extra
        v_tile = pl.BlockSpec((1, tm, C), lambda i: (2, i, 0))
        in_specs += [full(v1), full(v2), full(extra[2]), v_tile, v_tile]
        out_shape.append(jax.ShapeDtypeStruct((M, C), F32))
        out_specs.append(tile)
    return pl.pallas_call(
        functools.partial(_lora_kernel, has_v=vmix is not None),
        out_shape=out_shape,
        grid=(M // tm,),
        in_specs=in_specs,
        out_specs=out_specs,
        compiler_params=_cparams(("parallel",)),
        name="rwkv_lora",
    )(*args)


def _wkv_masks():
    L, Q = WKV_CHUNK, QUAD
    idx = jnp.arange(Q)
    t = jnp.arange(L)[:, None]
    s = (idx % L)[None, :]
    tri = (t >= jnp.arange(L)[None, :]).astype(BF16)
    strict = (t > s).astype(F32)
    incl = (t >= s).astype(F32)
    block = ((idx[:, None] // L) == (idx[None, :] // L)).astype(BF16)
    return tri, strict, incl, block


def _row_block_sum(z):
    L = WKV_CHUNK
    out = z[0:L]
    for h in range(1, HEADS_PER_QUAD):
        out = out + z[h * L:(h + 1) * L]
    return out


def _block_diag(z, block):
    return jnp.concatenate([z.astype(BF16)] * HEADS_PER_QUAD, axis=0) * block


def _head_sums(zs, block):
    n = zs[0].shape[0]
    parts = []
    for z in zs:
        parts += list(_split2(z))
    out = jnp.dot(jnp.concatenate(parts, axis=0), block, preferred_element_type=F32)
    return [out[2 * i * n:(2 * i + 1) * n] + out[(2 * i + 1) * n:(2 * i + 2) * n] for i in range(len(zs))]


def _wkv_par_kernel(r_ref, k_ref, v_ref, lw_ref, a_ref, kk_w_ref, ka_w_ref, rk_w_ref,
                    tri_ref, strict_ref, incl_ref, block_ref,
                    rp_ref, p2_ref, g_out_ref, h_out_ref, wl_ref, bonus_ref, *, chunks):
    L = WKV_CHUNK
    tri = tri_ref[...]
    block = block_ref[...]
    block_f32 = block.astype(F32)
    strict = strict_ref[...] > 0.5
    incl = incl_ref[...] > 0.5
    eye = incl_ref[...] - strict_ref[...]

    Q = QUAD
    each = range(chunks)
    rows = [slice(c * L, (c + 1) * L) for c in each]
    r = [r_ref[0, rw, :] for rw in rows]
    k = [k_ref[0, rw, :] for rw in rows]
    v = [v_ref[0, rw, :] for rw in rows]
    lw = [lw_ref[rw, :] for rw in rows]
    a_sig = [a_ref[rw, :] for rw in rows]

    parts = []
    for c in each:
        parts += list(_split3(lw[c]))
    sums = jnp.dot(tri, jnp.concatenate(parts, axis=1), preferred_element_type=F32)
    cum = [sums[:, 3 * c * Q:(3 * c + 1) * Q] + sums[:, (3 * c + 1) * Q:(3 * c + 2) * Q]
           + sums[:, (3 * c + 2) * Q:(3 * c + 3) * Q] for c in each]

    k_raw = [k[c] * kk_w_ref[...] for c in each]
    k_mod = [k[c] * (1.0 + (a_sig[c] - 1.0) * ka_w_ref[...]) for c in each]
    head = _head_sums([k_raw[c] * k_raw[c] for c in each]
                      + [r[c] * k_mod[c] * rk_w_ref[...] for c in each], block)
    a_t, r_t, b_end, k_end, n_ab, n_ak, n_rb, n_rk = [], [], [], [], [], [], [], []
    for c in each:
        cum_end = cum[c][L - 1:L, :]
        w_inv = jnp.exp(-cum[c])
        w_to_end = jnp.exp(cum_end - cum[c])
        kk = k_raw[c] / jnp.maximum(jnp.sqrt(head[c]), 1e-12)
        b = kk * a_sig[c]
        a_t.append(-kk * jnp.exp(cum[c] - lw[c]))
        r_t.append(r[c] * jnp.exp(cum[c]))
        b_end.append(b * w_to_end)
        k_end.append(k_mod[c] * w_to_end)
        wl_ref[c] = jnp.exp(cum_end)
        bonus_ref[rows[c], :] = head[chunks + c] * v[c]
        rhs = jnp.concatenate([_block_diag(b * w_inv, block).T, _block_diag(k_mod[c] * w_inv, block).T], axis=1)
        by = _dot(jnp.concatenate([a_t[c], r_t[c]], axis=0), rhs)
        n_ab.append(jnp.where(strict, by[:L, :Q], 0.0))
        n_ak.append(jnp.where(strict, by[:L, Q:], 0.0))
        n_rb.append(jnp.where(incl, by[L:, :Q], 0.0))
        n_rk.append(jnp.where(incl, by[L:, Q:], 0.0))

    by_v = [_dot(jnp.concatenate([n_ak[c], n_rk[c]], axis=0), _block_diag(v[c], block)) for c in each]
    p1 = [z[:L] for z in by_v]
    p2_v = [z[L:] for z in by_v]

    power = list(n_ab)
    t_inv = [eye + n_ab[c] for c in each]
    span = 2
    while span < L:
        for c in each:
            power_block = _block_diag(power[c], block)
            if span == 2:
                power[c] = _dot(power[c], power_block)
            else:
                both = _dot(jnp.concatenate([power[c], t_inv[c]], axis=0), power_block)
                power[c] = both[:L]
                t_inv[c] = t_inv[c] + both[L:]
        span *= 2
    for c in each:
        t_inv[c] = t_inv[c] + _dot(t_inv[c], _block_diag(power[c], block))

    read_inv = [_dot(n_rb[c], _block_diag(t_inv[c], block)) for c in each]
    solved = [_dot(jnp.concatenate([t_inv[c], read_inv[c]], axis=0),
                   jnp.concatenate([_block_diag(a_t[c], block), _block_diag(p1[c], block)], axis=1))
              for c in each]
    a_p = [z[:L, :Q] for z in solved]
    p1_p = [z[:L, Q:] for z in solved]
    for c in each:
        rp_ref[rows[c], :] = (r_t[c] + solved[c][L:, :Q]).astype(rp_ref.dtype)
        p2_ref[rows[c], :] = solved[c][L:, Q:] + p2_v[c]
    for c in each:
        g_map = _row_block_sum(_dot_tn(a_p[c], b_end[c]) * block_f32)
        h_map = _row_block_sum(_dot_tn(jnp.concatenate([p1_p[c], v[c]], axis=0),
                                       jnp.concatenate([b_end[c], k_end[c]], axis=0)) * block_f32)
        g_out_ref[rows[c], :] = g_map.astype(g_out_ref.dtype)
        h_out_ref[rows[c], :] = h_map


def _wkv_par(rkv, v_src, lw, a_sig, k_k, k_a, r_k, masks, *, chunks):
    _, M, C = rkv.shape
    v3, v_idx = v_src
    L, Q = WKV_CHUNK, QUAD
    rows = chunks * L
    tile = pl.BlockSpec((rows, Q), lambda i, q: (i, q))
    rk_tile = lambda n: pl.BlockSpec((1, rows, Q), lambda i, q, n=n: (n, i, q))
    vec = pl.BlockSpec((1, Q), lambda i, q: (0, q))
    const = lambda arr: pl.BlockSpec(arr.shape, lambda i, q: (0,) * arr.ndim)
    out_shape = [jax.ShapeDtypeStruct((M, C), BF16), jax.ShapeDtypeStruct((M, C), F32),
                 jax.ShapeDtypeStruct((M, C), BF16), jax.ShapeDtypeStruct((M, C), F32),
                 jax.ShapeDtypeStruct((M // L, 1, C), F32), jax.ShapeDtypeStruct((M, C), F32)]
    out_specs = [tile, tile, tile, tile, pl.BlockSpec((chunks, 1, Q), lambda i, q: (i, 0, q)), tile]
    return pl.pallas_call(
        functools.partial(_wkv_par_kernel, chunks=chunks),
        out_shape=out_shape,
        grid=(M // rows, C // Q),
        in_specs=[rk_tile(0), rk_tile(1), rk_tile(v_idx), tile, tile, vec, vec, vec] + [const(m) for m in masks],
        out_specs=out_specs,
        compiler_params=_cparams(("parallel", "parallel")),
        name="wkv_par",
    )(rkv, rkv, v3, lw, a_sig, k_k.reshape(1, C), k_a.reshape(1, C), r_k.reshape(1, C), *masks)


def _wkv_seq_kernel(rp_ref, p2_ref, gm_ref, hm_ref, wl_ref, bonus_ref, gate_ref, lng_ref, lnb_ref,
                    block_ref, o_ref, s_ref, *, chunks, quads):
    L, Q = WKV_CHUNK, QUAD

    @pl.when(pl.program_id(1) == 0)
    def _():
        s_ref[...] = jnp.zeros_like(s_ref)

    block = block_ref[...]
    inv_n = 1.0 / HEAD_SIZE

    def chunk_body(c, carry):
        rows = pl.ds(pl.multiple_of(c * L, L), L)
        each = range(quads)
        lanes = [slice(q * Q, (q + 1) * Q) for q in each]
        s = [s_ref[q] for q in each]
        y = [_dot_nt(rp_ref[rows, lanes[q]], _block_diag(s[q], block)) + p2_ref[rows, lanes[q]] for q in each]
        for q in each:
            s_ref[q] = (s[q] * wl_ref[c, :, lanes[q]] + _dot(s[q], _block_diag(gm_ref[rows, lanes[q]], block))
                        + hm_ref[rows, lanes[q]])
        mean = _head_sums(y, block)
        d = [y[q] - mean[q] * inv_n for q in each]
        var = _head_sums([z * z for z in d], block)
        for q in each:
            yn = d[q] * lax.rsqrt(var[q] * inv_n + GN_EPS) * lng_ref[:, lanes[q]] + lnb_ref[:, lanes[q]]
            out = (yn + bonus_ref[rows, lanes[q]]) * gate_ref[rows, lanes[q]].astype(F32)
            o_ref[rows, lanes[q]] = out.astype(o_ref.dtype)
        return carry

    lax.fori_loop(0, chunks, chunk_body, 0)


def _wkv_seq(rp, p2, gm, hm, wl, bonus, gate, lnx_g, lnx_b, block, *, chunks, seq_len):
    M, C = rp.shape
    L, Q = WKV_CHUNK, QUAD
    rows = chunks * L
    steps_per_seq = seq_len // rows
    quads = C // Q
    tile = pl.BlockSpec((rows, C), lambda b, i: (b * steps_per_seq + i, 0))
    vec = pl.BlockSpec((1, C), lambda b, i: (0, 0))
    return pl.pallas_call(
        functools.partial(_wkv_seq_kernel, chunks=chunks, quads=quads),
        out_shape=jax.ShapeDtypeStruct((M, C), BF16),
        grid=(M // seq_len, steps_per_seq),
        in_specs=[tile, tile, tile, tile,
                  pl.BlockSpec((chunks, 1, C), lambda b, i: (b * steps_per_seq + i, 0, 0)),
                  tile, tile, vec, vec,
                  pl.BlockSpec(block.shape, lambda b, i: (0, 0))],
        out_specs=tile,
        scratch_shapes=[pltpu.VMEM((quads, L, Q), F32)],
        compiler_params=_cparams(("parallel", "arbitrary")),
        name="wkv_seq",
    )(rp, p2, gm, hm, wl, bonus, gate, lnx_g.reshape(1, C), lnx_b.reshape(1, C), block)


def _conf_mid_kernel(u_ref, uh_ref, wdw_ref, bdw_ref, lng_ref, lnb_ref, o_ref,
                     u_scr, c_scr, *, tiles_per_seq):
    tm, C = o_ref.shape
    at_seq_start = (pl.program_id(0) % tiles_per_seq) == 0
    u_scr[0:CONV_HALO, :] = jnp.where(at_seq_start, 0.0, uh_ref[0].astype(F32))
    u_scr[CONV_HALO:CONV_HALO + tm, :] = u_ref[0].astype(F32)
    first_tap = CONV_HALO - (CONV_WIDTH - 1)

    def slab(s, carry):
        lanes = pl.ds(pl.multiple_of(s * LANES, LANES), LANES)
        window = u_scr[:, lanes]
        acc = jnp.zeros((tm, LANES), F32) + bdw_ref[:, lanes]
        for phase in range(SUBLANES_F32):
            shifted = window if phase == 0 else pltpu.roll(window, CONV_HALO + tm - phase, axis=0)
            for tap in range(CONV_WIDTH):
                start = first_tap + tap - phase
                if start % SUBLANES_F32 == 0:
                    acc = acc + shifted[start:start + tm] * wdw_ref[tap:tap + 1, lanes]
        c_scr[:, lanes] = acc
        return carry

    lax.fori_loop(0, C // LANES, slab, 0)
    y = c_scr[...]
    mu = jnp.mean(y, axis=-1, keepdims=True)
    d = y - mu
    var = jnp.mean(d * d, axis=-1, keepdims=True)
    yn = d * lax.rsqrt(var + LN_EPS) * lng_ref[...] + lnb_ref[...]
    o_ref[...] = _silu(yn).astype(o_ref.dtype)


def _conf_mid(u, w_dw, b_dw, ln_g, ln_b, *, tm, seq_len):
    _, M, C = u.shape
    tiles_per_seq = seq_len // tm
    halo_per_tile = tm // CONV_HALO
    halo_idx = lambda i: jnp.maximum(i * halo_per_tile - 1, 0)
    vec = pl.BlockSpec((1, C), lambda i: (0, 0))
    return pl.pallas_call(
        functools.partial(_conf_mid_kernel, tiles_per_seq=tiles_per_seq),
        out_shape=jax.ShapeDtypeStruct((M, C), BF16),
        grid=(M // tm,),
        in_specs=[pl.BlockSpec((1, tm, C), lambda i: (0, i, 0)),
                  pl.BlockSpec((1, CONV_HALO, C), lambda i: (0, halo_idx(i), 0)),
                  pl.BlockSpec(w_dw.shape, lambda i: (0, 0)), vec, vec, vec],
        out_specs=pl.BlockSpec((tm, C), lambda i: (i, 0)),
        scratch_shapes=[pltpu.VMEM((CONV_HALO + tm, C), F32), pltpu.VMEM((tm, C), F32)],
        compiler_params=_cparams(("parallel",)),
        name="conformer_mid",
    )(u, u, w_dw, b_dw.reshape(1, C), ln_g.reshape(1, C), ln_b.reshape(1, C))


def _ffn_up_kernel(x_ref, xh_ref, g_ref, mod_ref, wg_ref, wv_ref, wdw_ref, bdw_ref, o_ref,
                   h_scr, *gate_scrs, rows, tiles_per_seq):
    tm, tn = o_ref.shape
    sub = tm // len(gate_scrs)

    @pl.when(pl.program_id(1) == 0)
    def _():
        g = g_ref[...]
        at_seq_start = (pl.program_id(0) % tiles_per_seq) == 0
        h_halo = _rms_mod(xh_ref[...], g, mod_ref, rows)
        h_scr[0:FFN_HALO, :] = jnp.where(at_seq_start, 0.0, h_halo).astype(h_scr.dtype)
        h_scr[FFN_HALO:FFN_HALO + tm, :] = _rms_mod(x_ref[...], g, mod_ref, rows).astype(h_scr.dtype)

    vals = []
    for n, gate_scr in enumerate(gate_scrs):
        gate_scr[...] = jnp.dot(h_scr[n * sub:n * sub + FFN_HALO + sub, :], wg_ref[0],
                                preferred_element_type=F32)
        vals.append(jnp.dot(h_scr[FFN_HALO + n * sub:FFN_HALO + (n + 1) * sub, :], wv_ref[0],
                            preferred_element_type=F32))
    first_tap = FFN_HALO - (FFN_CONV_WIDTH - 1)
    for n, gate_scr in enumerate(gate_scrs):
        acc = jnp.zeros((sub, tn), F32) + bdw_ref[...]
        for tap in range(FFN_CONV_WIDTH):
            acc = acc + gate_scr[pl.ds(first_tap + tap, sub), :] * wdw_ref[tap:tap + 1, :]
        o_ref[n * sub:(n + 1) * sub, :] = (_silu(acc) * vals[n]).astype(o_ref.dtype)


def _ffn_up(x, g, mod_l, w_gv, layer, w_dw, b_dw, *, rows, tm, tn, row_splits, seq_len):
    M, C = x.shape
    F = w_gv.shape[-1] // 2
    tiles_per_seq = seq_len // tm
    halo_per_tile = tm // FFN_HALO
    return pl.pallas_call(
        functools.partial(_ffn_up_kernel, rows=rows, tiles_per_seq=tiles_per_seq),
        out_shape=jax.ShapeDtypeStruct((M, F), BF16),
        grid=(M // tm, F // tn),
        in_specs=[pl.BlockSpec((tm, C), lambda i, j: (i, 0)),
                  pl.BlockSpec((FFN_HALO, C), lambda i, j: (jnp.maximum(i * halo_per_tile - 1, 0), 0)),
                  pl.BlockSpec((1, C), lambda i, j: (0, 0)),
                  pl.BlockSpec((1,) + mod_l.shape[1:], lambda i, j: (i // tiles_per_seq, 0, 0)),
                  pl.BlockSpec((1, C, tn), lambda i, j: (layer, 0, j)),
                  pl.BlockSpec((1, C, tn), lambda i, j: (layer, 0, F // tn + j)),
                  pl.BlockSpec((FFN_CONV_WIDTH, tn), lambda i, j: (0, j)),
                  pl.BlockSpec((1, tn), lambda i, j: (0, j))],
        out_specs=pl.BlockSpec((tm, tn), lambda i, j: (i, j)),
        scratch_shapes=([pltpu.VMEM((FFN_HALO + tm, C), BF16)]
                        + [pltpu.VMEM((FFN_HALO + tm // row_splits, tn), F32)] * row_splits),
        compiler_params=_cparams(("parallel", "arbitrary")),
        name="ffn_up",
    )(x, x, g.reshape(1, C), mod_l, w_gv, w_gv, w_dw, b_dw.reshape(1, F))


def _pad_to(z, axis, size):
    pad = [(0, 0)] * z.ndim
    pad[axis] = (0, size - z.shape[axis])
    return jnp.pad(z, pad)


def _round_up(n, m):
    return (n + m - 1) // m * m


def _tiles(seq_len, C):
    big = min(1024, seq_len)
    return dict(
        norm=min(512, seq_len), rwkv_in=min(256, seq_len), lora=min(256, seq_len),
        gemm_m=min(512, seq_len), gemm_n=min(2048, C),
        down_m=big, down_n=512,
        glu_n=min(1024, C),
        conf=min(512, seq_len), ffn_m=big, ffn_n=512, ffn_row_splits=1,
        wkv_par_chunks=min(16, seq_len // WKV_CHUNK), wkv_seq_chunks=min(4, seq_len // WKV_CHUNK),
        mod_n=1024,
    )


def kernel(x, c, ada_w, ada_b, norm_mix_g, norm_ffn_g, rwkv_mu, rwkv_w_rkv, rwkv_w0, rwkv_w1, rwkv_w2, rwkv_a0, rwkv_a1, rwkv_a2, rwkv_v0, rwkv_v1, rwkv_v2, rwkv_g1, rwkv_g2, rwkv_k_k, rwkv_k_a, rwkv_r_k, rwkv_lnx_g, rwkv_lnx_b, rwkv_w_o, conv_w_pw1, conv_b_pw1, conv_w_dw, conv_b_dw, conv_ln_g, conv_ln_b, conv_w_pw2, conv_b_pw2, ffn_w_up, ffn_w_dw, ffn_b_dw, ffn_w_down, final_norm_g):
    B, T, C = x.shape
    M = B * T
    depth = ada_w.shape[0]
    F = ffn_w_dw.shape[-1]
    t = _tiles(T, C)
    Fp = _round_up(F, t["ffn_n"])
    lora_pad = lambda w, axis: _pad_to(w, axis, _round_up(w.shape[axis], LANES)).astype(BF16)

    xf = x.reshape(M, C)
    c_pad = _pad_to(c, 0, SUBLANES_F32)
    mod = _adaln_mod(c_pad, ada_w, ada_b, tn=min(t["mod_n"], 6 * C))
    mod = mod[:, :B].reshape(depth, B, 6, C)
    masks = _wkv_masks()
    w_gv = _pad_to(ffn_w_up.reshape(depth, C, 2, F), 3, Fp).astype(BF16).reshape(depth, C, 2 * Fp)
    w_down = _pad_to(ffn_w_down, 1, Fp).astype(BF16)

    rkv_first = None
    for i in range(depth):
        mod_l = mod[i]
        j = i // 2
        if i % 2 == 0:
            hx = _rwkv_in(xf, norm_mix_g[i], mod_l, rows=(0, 1), tm=t["rwkv_in"], seq_len=T)
            mu = rwkv_mu[j]
            rkv = _gemm(hx, rwkv_w_rkv[j].astype(BF16), mu=mu[:3, None, :], out_dtype=F32,
                        tm=t["gemm_m"], tn=t["gemm_n"])
            lora_w = (lora_pad(rwkv_w1[j], 1), lora_pad(rwkv_w2[j], 0), lora_pad(rwkv_a1[j], 1),
                      lora_pad(rwkv_a2[j], 0), rwkv_g1[j].astype(BF16), rwkv_g2[j].astype(BF16),
                      rwkv_w0[j], rwkv_a0[j])
            if rkv_first is None:
                lw, a_sig, gate = _rwkv_lora(hx, mu, *lora_w, tm=t["lora"])
                v_src = (rkv, 2)
                rkv_first = rkv
            else:
                vmix = (lora_pad(rwkv_v1[j - 1], 1), lora_pad(rwkv_v2[j - 1], 0), rwkv_v0[j - 1], rkv, rkv_first)
                lw, a_sig, gate, v = _rwkv_lora(hx, mu, *lora_w, vmix=vmix, tm=t["lora"])
                v_src = (v[None], 0)
            rp, p2, gm, hm, wl, bonus = _wkv_par(rkv, v_src, lw, a_sig, rwkv_k_k[j], rwkv_k_a[j],
                                                 rwkv_r_k[j].reshape(C), masks, chunks=t["wkv_par_chunks"])
            y = _wkv_seq(rp, p2, gm, hm, wl, bonus, gate, rwkv_lnx_g[j], rwkv_lnx_b[j], masks[3],
                         chunks=t["wkv_seq_chunks"], seq_len=T)
            xf = _gemm(y[None], rwkv_w_o[j].astype(BF16)[None], res=xf, mod_l=mod_l, gate_row=2,
                       out_dtype=F32, tm=t["gemm_m"], tn=t["gemm_n"], seq_len=T)[0]
        else:
            u = _gemm(xf[None], conv_w_pw1[j].astype(BF16)[None], bias=conv_b_pw1[j][None], mod_l=mod_l,
                      norm=(norm_mix_g[i], (0, 1)), glu=True, out_dtype=BF16, tm=t["gemm_m"], tn=t["glu_n"],
                      seq_len=T)
            u = _conf_mid(u, conv_w_dw[j], conv_b_dw[j], conv_ln_g[j], conv_ln_b[j], tm=t["conf"], seq_len=T)
            xf = _gemm(u[None], conv_w_pw2[j].astype(BF16)[None], bias=conv_b_pw2[j][None], res=xf,
                       mod_l=mod_l, gate_row=2, out_dtype=F32, tm=t["gemm_m"], tn=t["gemm_n"], seq_len=T)[0]

        z = _ffn_up(xf, norm_ffn_g[i], mod_l, w_gv, i, _pad_to(ffn_w_dw[i], 1, Fp), _pad_to(ffn_b_dw[i], 0, Fp),
                    rows=(3, 4), tm=t["ffn_m"], tn=t["ffn_n"], row_splits=t["ffn_row_splits"], seq_len=T)
        xf = _gemm(z[None], w_down, w_lead=i, res=xf, mod_l=mod_l, gate_row=5,
                   out_dtype=F32, tm=t["down_m"], tn=t["down_n"], seq_len=T)[0]

    out = _norm_mod(xf, final_norm_g, mod[0], rows=None, out_dtype=F32, tm=t["norm"], seq_len=T)
    return out.reshape(B, T, C)
```

```python
import functools
import math

import jax
import jax.numpy as jnp
from jax import lax
from jax.experimental import pallas as pl
from jax.experimental.pallas import tpu as pltpu

F32 = jnp.float32
BF16 = jnp.bfloat16

HEAD_SIZE = 64
RMS_EPS = 1e-6
LN_EPS = 1e-5
GN_EPS = 64e-5
DECAY_SCALE = -math.exp(-0.5)
CONV_WIDTH = 31
FFN_CONV_WIDTH = 3

LANES = 128
SUBLANES_F32 = 8
SUBLANES_BF16 = 16
MXU_DIM = 256
VMEM_LIMIT_BYTES = 56 * 1024 * 1024

WKV_CHUNK = HEAD_SIZE
QUAD = MXU_DIM
HEADS_PER_QUAD = QUAD // HEAD_SIZE
CONV_HALO = 32
FFN_HALO = 16


def _cparams(semantics):
    return pltpu.CompilerParams(dimension_semantics=semantics,
                                vmem_limit_bytes=VMEM_LIMIT_BYTES)


def _sigmoid(z):
    return 0.5 * jnp.tanh(0.5 * z) + 0.5


def _silu(z):
    h = 0.5 * z
    return h * jnp.tanh(h) + h


def _dot(a, b):
    return jnp.dot(a.astype(BF16), b.astype(BF16), preferred_element_type=F32)


def _dot_nt(a, b):
    return lax.dot_general(a.astype(BF16), b.astype(BF16), (((1,), (1,)), ((), ())),
                           preferred_element_type=F32)


def _dot_tn(a, b):
    return lax.dot_general(a.astype(BF16), b.astype(BF16), (((0,), (0,)), ((), ())),
                           preferred_element_type=F32)


def _split2(z):
    hi = z.astype(BF16)
    lo = (z - hi.astype(F32)).astype(BF16)
    return hi, lo


def _split3(z):
    hi = z.astype(BF16)
    r1 = z - hi.astype(F32)
    mid = r1.astype(BF16)
    lo = (r1 - mid.astype(F32)).astype(BF16)
    return hi, mid, lo


def _mod_kernel(c_ref, w_ref, b_ref, o_ref):
    c = c_ref[...]
    c_act = _silu(c).astype(BF16)
    o_ref[0] = jnp.dot(c_act, w_ref[0].astype(BF16), preferred_element_type=F32) + b_ref[0]


def _adaln_mod(c_pad, ada_w, ada_b, *, tn):
    depth, C, n6 = ada_w.shape
    rows = c_pad.shape[0]
    return pl.pallas_call(
        _mod_kernel,
        out_shape=jax.ShapeDtypeStruct((depth, rows, n6), F32),
        grid=(depth, n6 // tn),
        in_specs=[pl.BlockSpec((rows, C), lambda l, j: (0, 0)),
                  pl.BlockSpec((1, C, tn), lambda l, j: (l, 0, j)),
                  pl.BlockSpec((1, 1, tn), lambda l, j: (l, 0, j))],
        out_specs=pl.BlockSpec((1, rows, tn), lambda l, j: (l, 0, j)),
        compiler_params=_cparams(("parallel", "parallel")),
        name="adaln_mod",
    )(c_pad, ada_w, ada_b.reshape(depth, 1, n6))


def _rms_mod(x, g, mod_ref, rows):
    y = x * lax.rsqrt(jnp.mean(x * x, axis=-1, keepdims=True) + RMS_EPS) * g
    if rows is not None:
        shift_row, scale_row = rows
        y = y * (1.0 + mod_ref[0, scale_row:scale_row + 1, :]) + mod_ref[0, shift_row:shift_row + 1, :]
    return y


def _norm_kernel(x_ref, g_ref, mod_ref, o_ref, *, rows):
    o_ref[...] = _rms_mod(x_ref[...], g_ref[...], mod_ref, rows).astype(o_ref.dtype)


def _norm_mod(x, g, mod_l, *, rows, out_dtype, tm, seq_len):
    M, C = x.shape
    tiles_per_seq = seq_len // tm
    return pl.pallas_call(
        functools.partial(_norm_kernel, rows=rows),
        out_shape=jax.ShapeDtypeStruct((M, C), out_dtype),
        grid=(M // tm,),
        in_specs=[pl.BlockSpec((tm, C), lambda i: (i, 0)),
                  pl.BlockSpec((1, C), lambda i: (0, 0)),
                  pl.BlockSpec((1,) + mod_l.shape[1:], lambda i: (i // tiles_per_seq, 0, 0))],
        out_specs=pl.BlockSpec((tm, C), lambda i: (i, 0)),
        compiler_params=_cparams(("parallel",)),
        name="norm_mod",
    )(x, g.reshape(1, C), mod_l)


def _rwkv_in_kernel(x_ref, xh_ref, g_ref, mod_ref, o_ref, *, rows, tiles_per_seq):
    i = pl.program_id(0)
    g = g_ref[...]
    h = _rms_mod(x_ref[...], g, mod_ref, rows)
    h_halo = _rms_mod(xh_ref[...], g, mod_ref, rows)
    at_seq_start = (i % tiles_per_seq) == 0
    prev_row = jnp.where(at_seq_start, 0.0, h_halo[SUBLANES_F32 - 1:SUBLANES_F32, :])
    rolled = pltpu.roll(h, 1, axis=0)
    row = lax.broadcasted_iota(jnp.int32, h.shape, 0)
    o_ref[0] = h.astype(o_ref.dtype)
    o_ref[1] = (jnp.where(row == 0, prev_row, rolled) - h).astype(o_ref.dtype)


def _rwkv_in(x, g, mod_l, *, rows, tm, seq_len):
    M, C = x.shape
    tiles_per_seq = seq_len // tm
    halo_blocks_per_tile = tm // SUBLANES_F32
    return pl.pallas_call(
        functools.partial(_rwkv_in_kernel, rows=rows, tiles_per_seq=tiles_per_seq),
        out_shape=jax.ShapeDtypeStruct((2, M, C), BF16),
        grid=(M // tm,),
        in_specs=[pl.BlockSpec((tm, C), lambda i: (i, 0)),
                  pl.BlockSpec((SUBLANES_F32, C),
                               lambda i: (jnp.maximum(i * halo_blocks_per_tile - 1, 0), 0)),
                  pl.BlockSpec((1, C), lambda i: (0, 0)),
                  pl.BlockSpec((1,) + mod_l.shape[1:], lambda i: (i // tiles_per_seq, 0, 0))],
        out_specs=pl.BlockSpec((2, tm, C), lambda i: (0, i, 0)),
        compiler_params=_cparams(("parallel",)),
        name="rwkv_in",
    )(x, x, g.reshape(1, C), mod_l)


def _lerp(hx_ref, mu_row):
    return hx_ref[0] + hx_ref[1] * mu_row.astype(BF16)


def _gemm_kernel(*refs, names, gate_row, norm_rows):
    r = dict(zip(names, refs))
    if "h_scr" in r:
        @pl.when(pl.program_id(2) == 0)
        def _():
            if norm_rows is not None:
                lhs0 = _rms_mod(r["x"][0], r["g"][...], r["mod_norm"], norm_rows)
            else:
                lhs0 = _lerp(r["x"], r["mu"][0])
            r["h_scr"][...] = lhs0.astype(BF16)

        lhs = r["h_scr"][...]
    else:
        lhs = r["x"][0]
    acc = jnp.dot(lhs, r["w"][0], preferred_element_type=F32)
    if "bias" in r:
        acc = acc + r["bias"][0]
    if "w_glu" in r:
        acc = acc * _sigmoid(jnp.dot(lhs, r["w_glu"][0], preferred_element_type=F32) + r["bias_glu"][0])
    if gate_row is not None:
        acc = r["res"][...] + r["mod_gate"][0, gate_row:gate_row + 1, :] * acc
    r["out"][0] = acc.astype(r["out"].dtype)


def _gemm(x, w, *, bias=None, res=None, mod_l=None, gate_row=None, norm=None, mu=None, glu=False,
          out_dtype, tm, tn, seq_len=None):
    nx, M, K = x.shape
    nb, _, N = w.shape
    if glu:
        N = N // 2
    n_col = N // tn
    tiles_per_seq = None if seq_len is None else seq_len // tm
    derived_lhs = norm is not None or mu is not None
    if mu is not None:
        x_spec = pl.BlockSpec((2, tm, K), lambda n, i, j: (0, i, 0))
    elif nx > 1:
        x_spec = pl.BlockSpec((1, tm, K), lambda n, i, j: (n, i, 0))
    else:
        x_spec = pl.BlockSpec((1, tm, K), lambda n, i, j: (0, i, 0))
    operands = [("x", x, x_spec), ("w", w, pl.BlockSpec((1, K, tn), lambda n, i, j: (n, 0, j)))]
    if norm is not None:
        operands.append(("g", norm[0].reshape(1, K), pl.BlockSpec((1, K), lambda n, i, j: (0, 0))))
        operands.append(("mod_norm", mod_l,
                         pl.BlockSpec((1,) + mod_l.shape[1:], lambda n, i, j: (i // tiles_per_seq, 0, 0))))
    if mu is not None:
        operands.append(("mu", mu, pl.BlockSpec((1, 1, K), lambda n, i, j: (n, 0, 0))))
    if bias is not None:
        bias = bias.reshape(nb, 1, -1).astype(F32)
        operands.append(("bias", bias, pl.BlockSpec((1, 1, tn), lambda n, i, j: (n, 0, j))))
    if glu:
        operands.append(("w_glu", w, pl.BlockSpec((1, K, tn), lambda n, i, j: (n, 0, j + n_col))))
        operands.append(("bias_glu", bias, pl.BlockSpec((1, 1, tn), lambda n, i, j: (n, 0, j + n_col))))
    if gate_row is not None:
        operands.append(("res", res, pl.BlockSpec((tm, tn), lambda n, i, j: (i, j))))
        operands.append(("mod_gate", mod_l,
                         pl.BlockSpec((1, mod_l.shape[1], tn), lambda n, i, j: (i // tiles_per_seq, 0, j))))
    names = tuple(name for name, _, _ in operands) + ("out",) + (("h_scr",) if derived_lhs else ())
    return pl.pallas_call(
        functools.partial(_gemm_kernel, names=names, gate_row=gate_row,
                          norm_rows=None if norm is None else norm[1]),
        out_shape=jax.ShapeDtypeStruct((nb, M, N), out_dtype),
        grid=(nb, M // tm, n_col),
        in_specs=[spec for _, _, spec in operands],
        out_specs=pl.BlockSpec((1, tm, tn), lambda n, i, j: (n, i, j)),
        scratch_shapes=[pltpu.VMEM((tm, K), BF16)] if derived_lhs else [],
        compiler_params=_cparams(("parallel", "parallel", "arbitrary" if derived_lhs else "parallel")),
        name="gemm",
    )(*[arr for _, arr, _ in operands])


def _lora_kernel(*refs, has_v):
    (hx_ref, mu_ref, w1_ref, w2_ref, a1_ref, a2_ref, g1_ref, g2_ref, w0_ref, a0_ref) = refs[:10]
    pos = 10
    if has_v:
        v1_ref, v2_ref, v0_ref, v_ref, vf_ref = refs[pos:pos + 5]
        pos += 5
    lw_ref, a_ref, g_ref = refs[pos:pos + 3]
    pos += 3
    xw = _lerp(hx_ref, mu_ref[3:4, :])
    xa = _lerp(hx_ref, mu_ref[4:5, :])
    xg = _lerp(hx_ref, mu_ref[5:6, :])

    zw = w0_ref[...] + _dot(jnp.tanh(_dot(xw, w1_ref[...])), w2_ref[...])
    lw_ref[...] = DECAY_SCALE * _sigmoid(zw)
    a_ref[...] = _sigmoid(a0_ref[...] + _dot(_dot(xa, a1_ref[...]), a2_ref[...]))
    g_ref[...] = _dot(_sigmoid(_dot(xg, g1_ref[...])), g2_ref[...]).astype(g_ref.dtype)
    if has_v:
        vo_ref = refs[pos]
        xv = _lerp(hx_ref, mu_ref[2:3, :])
        mix = _sigmoid(v0_ref[...] + _dot(_dot(xv, v1_ref[...]), v2_ref[...]))
        v = v_ref[0]
        vo_ref[...] = v + (vf_ref[0] - v) * mix


def _rwkv_lora(hx, mu, w1, w2, a1, a2, g1, g2, w0, a0, *, vmix=None, tm):
    _, M, C = hx.shape
    full = lambda arr: pl.BlockSpec(arr.shape, lambda i: (0,) * arr.ndim)
    tile = pl.BlockSpec((tm, C), lambda i: (i, 0))
    vec = lambda z: z.reshape(1, C)
    args = [hx, mu, w1, w2, a1, a2, g1, g2, vec(w0), vec(a0)]
    in_specs = [pl.BlockSpec((2, tm, C), lambda i: (0, i, 0))] + [full(z) for z in args[1:]]
    out_shape = [jax.ShapeDtypeStruct((M, C), F32), jax.ShapeDtypeStruct((M, C), F32),
                 jax.ShapeDtypeStruct((M, C), BF16)]
    out_specs = [tile, tile, tile]
    if vmix is not None:
        v1, v2, v0, rkv, rkv_first = vmix
        extra = [v1, v2, vec(v0), rkv, rkv_first]
        args += extra
        v_tile = pl.BlockSpec((1, tm, C), lambda i: (2, i, 0))
        in_specs += [full(v1), full(v2), full(extra[2]), v_tile, v_tile]
        out_shape.append(jax.ShapeDtypeStruct((M, C), F32))
        out_specs.append(tile)
    return pl.pallas_call(
        functools.partial(_lora_kernel, has_v=vmix is not None),
        out_shape=out_shape,
        grid=(M // tm,),
        in_specs=in_specs,
        out_specs=out_specs,
        compiler_params=_cparams(("parallel",)),
        name="rwkv_lora",
    )(*args)


def _wkv_masks():
    L, Q = WKV_CHUNK, QUAD
    idx = jnp.arange(Q)
    t = jnp.arange(L)[:, None]
    s = (idx % L)[None, :]
    tri = (t >= jnp.arange(L)[None, :]).astype(BF16)
    strict = (t > s).astype(F32)
    incl = (t >= s).astype(F32)
    block = ((idx[:, None] // L) == (idx[None, :] // L)).astype(BF16)
    return tri, strict, incl, block


def _row_block_sum(z):
    L = WKV_CHUNK
    out = z[0:L]
    for h in range(1, HEADS_PER_QUAD):
        out = out + z[h * L:(h + 1) * L]
    return out


def _block_diag(z, block):
    return jnp.concatenate([z.astype(BF16)] * HEADS_PER_QUAD, axis=0) * block


def _head_sums(zs, block):
    n = zs[0].shape[0]
    parts = []
    for z in zs:
        parts += list(_split2(z))
    out = jnp.dot(jnp.concatenate(parts, axis=0), block, preferred_element_type=F32)
    return [out[2 * i * n:(2 * i + 1) * n] + out[(2 * i + 1) * n:(2 * i + 2) * n] for i in range(len(zs))]


def _wkv_par_kernel(r_ref, k_ref, v_ref, lw_ref, a_ref, kk_w_ref, ka_w_ref, rk_w_ref,
                    tri_ref, strict_ref, incl_ref, block_ref,
                    rp_ref, p2_ref, g_out_ref, h_out_ref, wl_ref, bonus_ref, *, chunks):
    L = WKV_CHUNK
    tri = tri_ref[...]
    block = block_ref[...]
    block_f32 = block.astype(F32)
    strict = strict_ref[...] > 0.5
    incl = incl_ref[...] > 0.5
    eye = incl_ref[...] - strict_ref[...]

    Q = QUAD
    each = range(chunks)
    rows = [slice(c * L, (c + 1) * L) for c in each]
    r = [r_ref[0, rw, :] for rw in rows]
    k = [k_ref[0, rw, :] for rw in rows]
    v = [v_ref[0, rw, :] for rw in rows]
    lw = [lw_ref[rw, :] for rw in rows]
    a_sig = [a_ref[rw, :] for rw in rows]

    parts = []
    for c in each:
        parts += list(_split3(lw[c]))
    sums = jnp.dot(tri, jnp.concatenate(parts, axis=1), preferred_element_type=F32)
    cum = [sums[:, 3 * c * Q:(3 * c + 1) * Q] + sums[:, (3 * c + 1) * Q:(3 * c + 2) * Q]
           + sums[:, (3 * c + 2) * Q:(3 * c + 3) * Q] for c in each]

    k_raw = [k[c] * kk_w_ref[...] for c in each]
    k_mod = [k[c] * (1.0 + (a_sig[c] - 1.0) * ka_w_ref[...]) for c in each]
    head = _head_sums([k_raw[c] * k_raw[c] for c in each]
                      + [r[c] * k_mod[c] * rk_w_ref[...] for c in each], block)
    a_t, r_t, b_end, k_end, n_ab, n_ak, n_rb, n_rk = [], [], [], [], [], [], [], []
    for c in each:
        cum_end = cum[c][L - 1:L, :]
        w_inv = jnp.exp(-cum[c])
        w_to_end = jnp.exp(cum_end - cum[c])
        kk = k_raw[c] / jnp.maximum(jnp.sqrt(head[c]), 1e-12)
        b = kk * a_sig[c]
        a_t.append(-kk * jnp.exp(cum[c] - lw[c]))
        r_t.append(r[c] * jnp.exp(cum[c]))
        b_end.append(b * w_to_end)
        k_end.append(k_mod[c] * w_to_end)
        wl_ref[c] = jnp.exp(cum_end)
        bonus_ref[rows[c], :] = head[chunks + c] * v[c]
        rhs = jnp.concatenate([_block_diag(b * w_inv, block).T, _block_diag(k_mod[c] * w_inv, block).T], axis=1)
        by = _dot(jnp.concatenate([a_t[c], r_t[c]], axis=0), rhs)
        n_ab.append(jnp.where(strict, by[:L, :Q], 0.0))
        n_ak.append(jnp.where(strict, by[:L, Q:], 0.0))
        n_rb.append(jnp.where(incl, by[L:, :Q], 0.0))
        n_rk.append(jnp.where(incl, by[L:, Q:], 0.0))

    by_v = [_dot(jnp.concatenate([n_ak[c], n_rk[c]], axis=0), _block_diag(v[c], block)) for c in each]
    p1 = [z[:L] for z in by_v]
    p2_v = [z[L:] for z in by_v]

    power = list(n_ab)
    t_inv = [eye + n_ab[c] for c in each]
    span = 2
    while span < L:
        for c in each:
            power_block = _block_diag(power[c], block)
            if span == 2:
                power[c] = _dot(power[c], power_block)
            else:
                both = _dot(jnp.concatenate([power[c], t_inv[c]], axis=0), power_block)
                power[c] = both[:L]
                t_inv[c] = t_inv[c] + both[L:]
        span *= 2
    for c in each:
        t_inv[c] = t_inv[c] + _dot(t_inv[c], _block_diag(power[c], block))

    read_inv = [_dot(n_rb[c], _block_diag(t_inv[c], block)) for c in each]
    solved = [_dot(jnp.concatenate([t_inv[c], read_inv[c]], axis=0),
                   jnp.concatenate([_block_diag(a_t[c], block), _block_diag(p1[c], block)], axis=1))
              for c in each]
    a_p = [z[:L, :Q] for z in solved]
    p1_p = [z[:L, Q:] for z in solved]
    for c in each:
        rp_ref[rows[c], :] = (r_t[c] + solved[c][L:, :Q]).astype(rp_ref.dtype)
        p2_ref[rows[c], :] = solved[c][L:, Q:] + p2_v[c]
    for c in each:
        g_map = _row_block_sum(_dot_tn(a_p[c], b_end[c]) * block_f32)
        h_map = _row_block_sum(_dot_tn(jnp.concatenate([p1_p[c], v[c]], axis=0),
                                       jnp.concatenate([b_end[c], k_end[c]], axis=0)) * block_f32)
        g_out_ref[rows[c], :] = g_map.astype(g_out_ref.dtype)
        h_out_ref[rows[c], :] = h_map


def _wkv_par(rkv, v_src, lw, a_sig, k_k, k_a, r_k, masks, *, chunks):
    _, M, C = rkv.shape
    v3, v_idx = v_src
    L, Q = WKV_CHUNK, QUAD
    rows = chunks * L
    tile = pl.BlockSpec((rows, Q), lambda i, q: (i, q))
    rk_tile = lambda n: pl.BlockSpec((1, rows, Q), lambda i, q, n=n: (n, i, q))
    vec = pl.BlockSpec((1, Q), lambda i, q: (0, q))
    const = lambda arr: pl.BlockSpec(arr.shape, lambda i, q: (0,) * arr.ndim)
    out_shape = [jax.ShapeDtypeStruct((M, C), BF16), jax.ShapeDtypeStruct((M, C), F32),
                 jax.ShapeDtypeStruct((M, C), BF16), jax.ShapeDtypeStruct((M, C), F32),
                 jax.ShapeDtypeStruct((M // L, 1, C), F32), jax.ShapeDtypeStruct((M, C), F32)]
    out_specs = [tile, tile, tile, tile, pl.BlockSpec((chunks, 1, Q), lambda i, q: (i, 0, q)), tile]
    return pl.pallas_call(
        functools.partial(_wkv_par_kernel, chunks=chunks),
        out_shape=out_shape,
        grid=(M // rows, C // Q),
        in_specs=[rk_tile(0), rk_tile(1), rk_tile(v_idx), tile, tile, vec, vec, vec] + [const(m) for m in masks],
        out_specs=out_specs,
        compiler_params=_cparams(("parallel", "parallel")),
        name="wkv_par",
    )(rkv, rkv, v3, lw, a_sig, k_k.reshape(1, C), k_a.reshape(1, C), r_k.reshape(1, C), *masks)


def _wkv_seq_kernel(rp_ref, p2_ref, gm_ref, hm_ref, wl_ref, bonus_ref, gate_ref, lng_ref, lnb_ref,
                    block_ref, o_ref, s_ref, *, chunks, quads):
    L, Q = WKV_CHUNK, QUAD

    @pl.when(pl.program_id(1) == 0)
    def _():
        s_ref[...] = jnp.zeros_like(s_ref)

    block = block_ref[...]
    inv_n = 1.0 / HEAD_SIZE

    def chunk_body(c, carry):
        rows = pl.ds(pl.multiple_of(c * L, L), L)
        each = range(quads)
        lanes = [slice(q * Q, (q + 1) * Q) for q in each]
        s = [s_ref[q] for q in each]
        y = [_dot_nt(rp_ref[rows, lanes[q]], _block_diag(s[q], block)) + p2_ref[rows, lanes[q]] for q in each]
        for q in each:
            s_ref[q] = (s[q] * wl_ref[c, :, lanes[q]] + _dot(s[q], _block_diag(gm_ref[rows, lanes[q]], block))
                        + hm_ref[rows, lanes[q]])
        mean = _head_sums(y, block)
        d = [y[q] - mean[q] * inv_n for q in each]
        var = _head_sums([z * z for z in d], block)
        for q in each:
            yn = d[q] * lax.rsqrt(var[q] * inv_n + GN_EPS) * lng_ref[:, lanes[q]] + lnb_ref[:, lanes[q]]
            out = (yn + bonus_ref[rows, lanes[q]]) * gate_ref[rows, lanes[q]].astype(F32)
            o_ref[rows, lanes[q]] = out.astype(o_ref.dtype)
        return carry

    lax.fori_loop(0, chunks, chunk_body, 0)


def _wkv_seq(rp, p2, gm, hm, wl, bonus, gate, lnx_g, lnx_b, block, *, chunks, seq_len):
    M, C = rp.shape
    L, Q = WKV_CHUNK, QUAD
    rows = chunks * L
    steps_per_seq = seq_len // rows
    quads = C // Q
    tile = pl.BlockSpec((rows, C), lambda b, i: (b * steps_per_seq + i, 0))
    vec = pl.BlockSpec((1, C), lambda b, i: (0, 0))
    return pl.pallas_call(
        functools.partial(_wkv_seq_kernel, chunks=chunks, quads=quads),
        out_shape=jax.ShapeDtypeStruct((M, C), BF16),
        grid=(M // seq_len, steps_per_seq),
        in_specs=[tile, tile, tile, tile,
                  pl.BlockSpec((chunks, 1, C), lambda b, i: (b * steps_per_seq + i, 0, 0)),
                  tile, tile, vec, vec,
                  pl.BlockSpec(block.shape, lambda b, i: (0, 0))],
        out_specs=tile,
        scratch_shapes=[pltpu.VMEM((quads, L, Q), F32)],
        compiler_params=_cparams(("parallel", "arbitrary")),
        name="wkv_seq",
    )(rp, p2, gm, hm, wl, bonus, gate, lnx_g.reshape(1, C), lnx_b.reshape(1, C), block)


def _conf_mid_kernel(u_ref, uh_ref, wdw_ref, bdw_ref, lng_ref, lnb_ref, o_ref,
                     u_scr, c_scr, *, tiles_per_seq):
    tm, C = o_ref.shape
    at_seq_start = (pl.program_id(0) % tiles_per_seq) == 0
    u_scr[0:CONV_HALO, :] = jnp.where(at_seq_start, 0.0, uh_ref[0].astype(F32))
    u_scr[CONV_HALO:CONV_HALO + tm, :] = u_ref[0].astype(F32)
    first_tap = CONV_HALO - (CONV_WIDTH - 1)

    def slab(s, carry):
        lanes = pl.ds(pl.multiple_of(s * LANES, LANES), LANES)
        window = u_scr[:, lanes]
        acc = jnp.zeros((tm, LANES), F32) + bdw_ref[:, lanes]
        for phase in range(SUBLANES_F32):
            shifted = window if phase == 0 else pltpu.roll(window, CONV_HALO + tm - phase, axis=0)
            for tap in range(CONV_WIDTH):
                start = first_tap + tap - phase
                if start % SUBLANES_F32 == 0:
                    acc = acc + shifted[start:start + tm] * wdw_ref[tap:tap + 1, lanes]
        c_scr[:, lanes] = acc
        return carry

    lax.fori_loop(0, C // LANES, slab, 0)
    y = c_scr[...]
    mu = jnp.mean(y, axis=-1, keepdims=True)
    d = y - mu
    var = jnp.mean(d * d, axis=-1, keepdims=True)
    yn = d * lax.rsqrt(var + LN_EPS) * lng_ref[...] + lnb_ref[...]
    o_ref[...] = _silu(yn).astype(o_ref.dtype)


def _conf_mid(u, w_dw, b_dw, ln_g, ln_b, *, tm, seq_len):
    _, M, C = u.shape
    tiles_per_seq = seq_len // tm
    halo_per_tile = tm // CONV_HALO
    halo_idx = lambda i: jnp.maximum(i * halo_per_tile - 1, 0)
    vec = pl.BlockSpec((1, C), lambda i: (0, 0))
    return pl.pallas_call(
        functools.partial(_conf_mid_kernel, tiles_per_seq=tiles_per_seq),
        out_shape=jax.ShapeDtypeStruct((M, C), BF16),
        grid=(M // tm,),
        in_specs=[pl.BlockSpec((1, tm, C), lambda i: (0, i, 0)),
                  pl.BlockSpec((1, CONV_HALO, C), lambda i: (0, halo_idx(i), 0)),
                  pl.BlockSpec(w_dw.shape, lambda i: (0, 0)), vec, vec, vec],
        out_specs=pl.BlockSpec((tm, C), lambda i: (i, 0)),
        scratch_shapes=[pltpu.VMEM((CONV_HALO + tm, C), F32), pltpu.VMEM((tm, C), F32)],
        compiler_params=_cparams(("parallel",)),
        name="conformer_mid",
    )(u, u, w_dw, b_dw.reshape(1, C), ln_g.reshape(1, C), ln_b.reshape(1, C))


def _ffn_up_kernel(x_ref, xh_ref, g_ref, mod_ref, wg_ref, wv_ref, wdw_ref, bdw_ref, o_ref,
                   h_scr, *gate_scrs, rows, tiles_per_seq):
    tm, tn = o_ref.shape
    sub = tm // len(gate_scrs)

    @pl.when(pl.program_id(1) == 0)
    def _():
        g = g_ref[...]
        at_seq_start = (pl.program_id(0) % tiles_per_seq) == 0
        h_halo = _rms_mod(xh_ref[...], g, mod_ref, rows)
        h_scr[0:FFN_HALO, :] = jnp.where(at_seq_start, 0.0, h_halo).astype(h_scr.dtype)
        h_scr[FFN_HALO:FFN_HALO + tm, :] = _rms_mod(x_ref[...], g, mod_ref, rows).astype(h_scr.dtype)

    vals = []
    for n, gate_scr in enumerate(gate_scrs):
        gate_scr[...] = jnp.dot(h_scr[n * sub:n * sub + FFN_HALO + sub, :], wg_ref[0],
                                preferred_element_type=F32)
        vals.append(jnp.dot(h_scr[FFN_HALO + n * sub:FFN_HALO + (n + 1) * sub, :], wv_ref[0],
                            preferred_element_type=F32))
    first_tap = FFN_HALO - (FFN_CONV_WIDTH - 1)
    for n, gate_scr in enumerate(gate_scrs):
        acc = jnp.zeros((sub, tn), F32) + bdw_ref[...]
        for tap in range(FFN_CONV_WIDTH):
            acc = acc + gate_scr[pl.ds(first_tap + tap, sub), :] * wdw_ref[tap:tap + 1, :]
        o_ref[n * sub:(n + 1) * sub, :] = (_silu(acc) * vals[n]).astype(o_ref.dtype)


def _ffn_up(x, g, mod_l, w_gv, w_dw, b_dw, *, rows, tm, tn, row_splits, seq_len):
    M, C = x.shape
    F = w_gv.shape[-1]
    tiles_per_seq = seq_len // tm
    halo_per_tile = tm // FFN_HALO
    return pl.pallas_call(
        functools.partial(_ffn_up_kernel, rows=rows, tiles_per_seq=tiles_per_seq),
        out_shape=jax.ShapeDtypeStruct((M, F), BF16),
        grid=(M // tm, F // tn),
        in_specs=[pl.BlockSpec((tm, C), lambda i, j: (i, 0)),
                  pl.BlockSpec((FFN_HALO, C), lambda i, j: (jnp.maximum(i * halo_per_tile - 1, 0), 0)),
                  pl.BlockSpec((1, C), lambda i, j: (0, 0)),
                  pl.BlockSpec((1,) + mod_l.shape[1:], lambda i, j: (i // tiles_per_seq, 0, 0)),
                  pl.BlockSpec((1, C, tn), lambda i, j: (0, 0, j)),
                  pl.BlockSpec((1, C, tn), lambda i, j: (1, 0, j)),
                  pl.BlockSpec((FFN_CONV_WIDTH, tn), lambda i, j: (0, j)),
                  pl.BlockSpec((1, tn), lambda i, j: (0, j))],
        out_specs=pl.BlockSpec((tm, tn), lambda i, j: (i, j)),
        scratch_shapes=([pltpu.VMEM((FFN_HALO + tm, C), BF16)]
                        + [pltpu.VMEM((FFN_HALO + tm // row_splits, tn), F32)] * row_splits),
        compiler_params=_cparams(("parallel", "arbitrary")),
        name="ffn_up",
    )(x, x, g.reshape(1, C), mod_l, w_gv, w_gv, w_dw, b_dw.reshape(1, F))


def _pad_to(z, axis, size):
    pad = [(0, 0)] * z.ndim
    pad[axis] = (0, size - z.shape[axis])
    return jnp.pad(z, pad)


def _round_up(n, m):
    return (n + m - 1) // m * m


def _tiles(seq_len, C):
    big = min(1024, seq_len)
    return dict(
        norm=min(512, seq_len), rwkv_in=min(256, seq_len), lora=min(256, seq_len),
        gemm_m=min(512, seq_len), gemm_n=min(2048, C),
        down_m=big, down_n=512,
        glu_n=min(1024, C),
        conf=min(512, seq_len), ffn_m=big, ffn_n=512, ffn_row_splits=1,
        wkv_par_chunks=min(16, seq_len // WKV_CHUNK), wkv_seq_chunks=min(4, seq_len // WKV_CHUNK),
        mod_n=1024,
    )


def kernel(x, c, ada_w, ada_b, norm_mix_g, norm_ffn_g, rwkv_mu, rwkv_w_rkv, rwkv_w0, rwkv_w1, rwkv_w2, rwkv_a0, rwkv_a1, rwkv_a2, rwkv_v0, rwkv_v1, rwkv_v2, rwkv_g1, rwkv_g2, rwkv_k_k, rwkv_k_a, rwkv_r_k, rwkv_lnx_g, rwkv_lnx_b, rwkv_w_o, conv_w_pw1, conv_b_pw1, conv_w_dw, conv_b_dw, conv_ln_g, conv_ln_b, conv_w_pw2, conv_b_pw2, ffn_w_up, ffn_w_dw, ffn_b_dw, ffn_w_down, final_norm_g):
    B, T, C = x.shape
    M = B * T
    depth = ada_w.shape[0]
    F = ffn_w_dw.shape[-1]
    t = _tiles(T, C)
    Fp = _round_up(F, t["ffn_n"])
    lora_pad = lambda w, axis: _pad_to(w, axis, _round_up(w.shape[axis], LANES)).astype(BF16)

    xf = x.reshape(M, C)
    c_pad = _pad_to(c, 0, SUBLANES_F32)
    mod = _adaln_mod(c_pad, ada_w, ada_b, tn=min(t["mod_n"], 6 * C))
    mod = mod[:, :B].reshape(depth, B, 6, C)
    masks = _wkv_masks()

    rkv_first = None
    for i in range(depth):
        mod_l = mod[i]
        j = i // 2
        if i % 2 == 0:
            hx = _rwkv_in(xf, norm_mix_g[i], mod_l, rows=(0, 1), tm=t["rwkv_in"], seq_len=T)
            mu = rwkv_mu[j]
            rkv = _gemm(hx, rwkv_w_rkv[j].astype(BF16), mu=mu[:3, None, :], out_dtype=F32,
                        tm=t["gemm_m"], tn=t["gemm_n"])
            lora_w = (lora_pad(rwkv_w1[j], 1), lora_pad(rwkv_w2[j], 0), lora_pad(rwkv_a1[j], 1),
                      lora_pad(rwkv_a2[j], 0), rwkv_g1[j].astype(BF16), rwkv_g2[j].astype(BF16),
                      rwkv_w0[j], rwkv_a0[j])
            if rkv_first is None:
                lw, a_sig, gate = _rwkv_lora(hx, mu, *lora_w, tm=t["lora"])
                v_src = (rkv, 2)
                rkv_first = rkv
            else:
                vmix = (lora_pad(rwkv_v1[j - 1], 1), lora_pad(rwkv_v2[j - 1], 0), rwkv_v0[j - 1], rkv, rkv_first)
                lw, a_sig, gate, v = _rwkv_lora(hx, mu, *lora_w, vmix=vmix, tm=t["lora"])
                v_src = (v[None], 0)
            rp, p2, gm, hm, wl, bonus = _wkv_par(rkv, v_src, lw, a_sig, rwkv_k_k[j], rwkv_k_a[j],
                                                 rwkv_r_k[j].reshape(C), masks, chunks=t["wkv_par_chunks"])
            y = _wkv_seq(rp, p2, gm, hm, wl, bonus, gate, rwkv_lnx_g[j], rwkv_lnx_b[j], masks[3],
                         chunks=t["wkv_seq_chunks"], seq_len=T)
            xf = _gemm(y[None], rwkv_w_o[j].astype(BF16)[None], res=xf, mod_l=mod_l, gate_row=2,
                       out_dtype=F32, tm=t["gemm_m"], tn=t["gemm_n"], seq_len=T)[0]
        else:
            u = _gemm(xf[None], conv_w_pw1[j].astype(BF16)[None], bias=conv_b_pw1[j][None], mod_l=mod_l,
                      norm=(norm_mix_g[i], (0, 1)), glu=True, out_dtype=BF16, tm=t["gemm_m"], tn=t["glu_n"],
                      seq_len=T)
            u = _conf_mid(u, conv_w_dw[j], conv_b_dw[j], conv_ln_g[j], conv_ln_b[j], tm=t["conf"], seq_len=T)
            xf = _gemm(u[None], conv_w_pw2[j].astype(BF16)[None], bias=conv_b_pw2[j][None], res=xf,
                       mod_l=mod_l, gate_row=2, out_dtype=F32, tm=t["gemm_m"], tn=t["gemm_n"], seq_len=T)[0]

        w_up = ffn_w_up[i]
        w_gv = jnp.stack([_pad_to(w_up[:, :F], 1, Fp), _pad_to(w_up[:, F:], 1, Fp)]).astype(BF16)
        z = _ffn_up(xf, norm_ffn_g[i], mod_l, w_gv, _pad_to(ffn_w_dw[i], 1, Fp), _pad_to(ffn_b_dw[i], 0, Fp),
                    rows=(3, 4), tm=t["ffn_m"], tn=t["ffn_n"], row_splits=t["ffn_row_splits"], seq_len=T)
        xf = _gemm(z[None], _pad_to(ffn_w_down[i], 0, Fp).astype(BF16)[None], res=xf, mod_l=mod_l, gate_row=5,
                   out_dtype=F32, tm=t["down_m"], tn=t["down_n"], seq_len=T)[0]

    out = _norm_mod(xf, final_norm_g, mod[0], rows=None, out_dtype=F32, tm=t["norm"], seq_len=T)
    return out.reshape(B, T, C)
```
